```python
import jax, jax.numpy as jnp
from jax import lax
import numpy as np

D_MODEL = 1024
BATCH = 8
SEQ = 2048
DEPTH = 2

MEM_LEN = 256
SB_HEADS = 8
SB_HEAD_DIM = 64
SB_WIDTH = SB_HEADS * SB_HEAD_DIM
SB_BLOCK = 128
ML_HEADS = 4
ML_HEAD_DIM = 128
ML_WIDTH = ML_HEADS * ML_HEAD_DIM
ML_CHUNK = 64
CONV_WIDTH = 4
MIX_WIDTH = SB_WIDTH + ML_WIDTH
N_IN = 3 * SB_WIDTH + 4 * ML_WIDTH + 2 * ML_HEADS
X_HEADS = 4
X_HEAD_DIM = D_MODEL // X_HEADS
D_FF = 2816
EPS = 1e-6

kernel_name = "hybrid_sb_mlstm_macaron_block"


def rms_norm(x, g):
    xf = x.astype(jnp.float32)
    y = xf * lax.rsqrt(jnp.mean(xf * xf, axis=-1, keepdims=True) + EPS)
    return (y * g.astype(jnp.float32)).astype(x.dtype)


def swiglu(x, w_gate, w_up, w_down):
    return (jax.nn.silu(x @ w_gate) * (x @ w_up)) @ w_down


def split_heads(a, n_heads, head_dim):
    b, s, _ = a.shape
    return a.reshape(b, s, n_heads, head_dim).transpose(0, 2, 1, 3)


def merge_heads(a):
    b, h, s, d = a.shape
    return a.transpose(0, 2, 1, 3).reshape(b, s, h * d)


def causal_depthwise_conv(u, w, b):
    k_width = w.shape[0]
    s = u.shape[1]
    up = jnp.pad(u, ((0, 0), (k_width - 1, 0), (0, 0)))
    y = b
    for j in range(k_width):
        y = y + up[:, j:j + s] * w[j]
    return y


def stick_breaking_attention(q, k, v):
    _, _, s_len, d = q.shape
    scale = d ** -0.5
    outs = []
    for blk in range(s_len // SB_BLOCK):
        q0 = blk * SB_BLOCK
        end = q0 + SB_BLOCK
        qb = q[:, :, q0:end].astype(jnp.float32)
        kb = k[:, :, :end].astype(jnp.float32)
        z = jnp.einsum('bhtd,bhsd->bhts', qb, kb) * scale
        t_idx = q0 + jnp.arange(SB_BLOCK)[:, None]
        s_idx = jnp.arange(end)[None, :]
        strict = s_idx < t_idx
        log_not = jnp.where(strict, jax.nn.log_sigmoid(-z), 0.0)
        later = lax.cumsum(log_not, axis=3, reverse=True) - log_not
        a = jnp.where(strict, jnp.exp(jax.nn.log_sigmoid(z) + later), 0.0)
        outs.append(jnp.einsum('bhts,bhsd->bhtd', a, v[:, :, :end].astype(jnp.float32)))
    return jnp.concatenate(outs, axis=2).astype(v.dtype)


def mlstm_chunkwise(q, k, v, i_pre, f_pre):
    b_sz, h_sz, s_len, d = q.shape
    n_chunks = s_len // ML_CHUNK
    qf = q.astype(jnp.float32) * (d ** -0.5)
    kf = k.astype(jnp.float32)
    vf = v.astype(jnp.float32)
    log_i = i_pre.astype(jnp.float32)
    log_f = jax.nn.log_sigmoid(f_pre.astype(jnp.float32))

    def to_chunks(a):
        return jnp.moveaxis(a.reshape(b_sz, h_sz, n_chunks, ML_CHUNK, *a.shape[3:]), 2, 0)

    causal = jnp.tril(jnp.ones((ML_CHUNK, ML_CHUNK), dtype=bool))

    def step(carry, xs):
        c_st, n_st, m_st = carry
        qc, kc, vc, li, lf = xs
        bcum = jnp.cumsum(lf, axis=-1)
        dmat = jnp.where(causal, bcum[..., :, None] - bcum[..., None, :] + li[..., None, :], -jnp.inf)
        inter = bcum + m_st[..., None]
        m_t = jnp.maximum(jnp.max(dmat, axis=-1), inter)
        w_intra = jnp.exp(dmat - m_t[..., None])
        w_inter = jnp.exp(inter - m_t)
        sc = jnp.einsum('bhtd,bhsd->bhts', qc, kc) * w_intra
        num = jnp.einsum('bhts,bhsd->bhtd', sc, vc) + w_inter[..., None] * jnp.einsum('bhtd,bhde->bhte', qc, c_st)
        den = jnp.sum(sc, axis=-1) + w_inter * jnp.einsum('bhtd,bhd->bht', qc, n_st)
        h = num / jnp.maximum(jnp.abs(den), jnp.exp(-m_t))[..., None]
        m_new = m_t[..., -1]
        w_state = jnp.exp(bcum[..., -1:] - bcum + li - m_new[..., None])
        decay = jnp.exp(bcum[..., -1] + m_st - m_new)
        c_new = decay[..., None, None] * c_st + jnp.einsum('bhs,bhsd,bhse->bhde', w_state, kc, vc)
        n_new = decay[..., None] * n_st + jnp.einsum('bhs,bhsd->bhd', w_state, kc)
        return (c_new, n_new, m_new), h

    init = (jnp.zeros((b_sz, h_sz, d, d), jnp.float32),
            jnp.zeros((b_sz, h_sz, d), jnp.float32),
            jnp.zeros((b_sz, h_sz), jnp.float32))
    xs = (to_chunks(qf), to_chunks(kf), to_chunks(vf), to_chunks(log_i), to_chunks(log_f))
    _, hs = lax.scan(step, init, xs)
    return jnp.moveaxis(hs, 0, 2).reshape(b_sz, h_sz, s_len, d).astype(q.dtype)


def memory_cross_attention(u, mem_n, w_xq, w_xk, w_xv, g_qnorm, g_knorm, w_xo):
    q = rms_norm(split_heads(u @ w_xq, X_HEADS, X_HEAD_DIM), g_qnorm)
    k = rms_norm(split_heads(mem_n @ w_xk, X_HEADS, X_HEAD_DIM), g_knorm)
    v = split_heads(mem_n @ w_xv, X_HEADS, X_HEAD_DIM)
    s = jnp.einsum('bhtd,bhmd->bhtm', q.astype(jnp.float32), k.astype(jnp.float32)) * (X_HEAD_DIM ** -0.5)
    p = jax.nn.softmax(s, axis=-1)
    o = jnp.einsum('bhtm,bhmd->bhtd', p, v.astype(jnp.float32)).astype(u.dtype)
    return merge_heads(o) @ w_xo


def hybrid_layer(x, mem, g_ff1, w_ff1_gate, w_ff1_up, w_ff1_down, g_mix, w_in, b_gate,
                 w_conv, b_conv, g_mlstm_head, w_out, g_xattn, g_mem, w_xq, w_xk, w_xv,
                 g_qnorm, g_knorm, w_xo, g_ff2, w_ff2_gate, w_ff2_up, w_ff2_down):
    x = x + 0.5 * swiglu(rms_norm(x, g_ff1), w_ff1_gate, w_ff1_up, w_ff1_down)

    u = rms_norm(x, g_mix)
    proj = u @ w_in
    sizes = (SB_WIDTH,) * 3 + (ML_WIDTH,) * 4 + (ML_HEADS, ML_HEADS)
    sb_q, sb_k, sb_v, ml_q, ml_k, ml_v, ml_o, ml_i, ml_f = jnp.split(
        proj, np.cumsum(sizes)[:-1].tolist(), axis=-1)

    sb = stick_breaking_attention(split_heads(sb_q, SB_HEADS, SB_HEAD_DIM),
                                  split_heads(sb_k, SB_HEADS, SB_HEAD_DIM),
                                  split_heads(sb_v, SB_HEADS, SB_HEAD_DIM))
    sb = merge_heads(sb)

    qk = jax.nn.silu(causal_depthwise_conv(jnp.concatenate([ml_q, ml_k], axis=-1), w_conv, b_conv))
    ml_qc, ml_kc = jnp.split(qk, 2, axis=-1)
    i_pre = (ml_i + b_gate[:ML_HEADS]).transpose(0, 2, 1)
    f_pre = (ml_f + b_gate[ML_HEADS:]).transpose(0, 2, 1)
    hm = mlstm_chunkwise(split_heads(ml_qc, ML_HEADS, ML_HEAD_DIM),
                         split_heads(ml_kc, ML_HEADS, ML_HEAD_DIM),
                         split_heads(ml_v, ML_HEADS, ML_HEAD_DIM), i_pre, f_pre)
    hm = rms_norm(hm, g_mlstm_head[:, None, :])
    ml = merge_heads(hm) * jax.nn.sigmoid(ml_o)

    x = x + jnp.concatenate([sb, ml], axis=-1) @ w_out

    x = x + memory_cross_attention(rms_norm(x, g_xattn), rms_norm(mem, g_mem),
                                   w_xq, w_xk, w_xv, g_qnorm, g_knorm, w_xo)

    x = x + 0.5 * swiglu(rms_norm(x, g_ff2), w_ff2_gate, w_ff2_up, w_ff2_down)
    return x


def setup_inputs(seed: int = 0) -> dict:
    key = jax.random.key(seed)
    ks = jax.random.split(key, 32)
    f32 = jnp.float32

    def w(k, shape, fan_in):
        return jax.random.normal(k, shape, f32) * (fan_in ** -0.5)

    def gain(k, shape):
        return 1.0 + 0.02 * jax.random.normal(k, shape, f32)

    L = DEPTH
    i_bias = 0.1 * jax.random.normal(ks[8], (L, ML_HEADS), f32)
    f_bias = 3.0 + 0.5 * jax.random.normal(ks[9], (L, ML_HEADS), f32)
    return {
        "x": jax.random.normal(ks[0], (BATCH, SEQ, D_MODEL), f32),
        "mem": jax.random.normal(ks[1], (BATCH, MEM_LEN, D_MODEL), f32),
        "g_ff1": gain(ks[2], (L, D_MODEL)),
        "w_ff1_gate": w(ks[3], (L, D_MODEL, D_FF), D_MODEL),
        "w_ff1_up": w(ks[4], (L, D_MODEL, D_FF), D_MODEL),
        "w_ff1_down": w(ks[5], (L, D_FF, D_MODEL), D_FF),
        "g_mix": gain(ks[6], (L, D_MODEL)),
        "w_in": w(ks[7], (L, D_MODEL, N_IN), D_MODEL),
        "b_gate": jnp.concatenate([i_bias, f_bias], axis=-1),
        "w_conv": w(ks[10], (L, CONV_WIDTH, 2 * ML_WIDTH), CONV_WIDTH),
        "b_conv": 0.02 * jax.random.normal(ks[11], (L, 2 * ML_WIDTH), f32),
        "g_mlstm_head": gain(ks[12], (L, ML_HEADS, ML_HEAD_DIM)),
        "w_out": w(ks[13], (L, MIX_WIDTH, D_MODEL), MIX_WIDTH),
        "g_xattn": gain(ks[14], (L, D_MODEL)),
        "g_mem": gain(ks[15], (L, D_MODEL)),
        "w_xq": w(ks[16], (L, D_MODEL, D_MODEL), D_MODEL),
        "w_xk": w(ks[17], (L, D_MODEL, D_MODEL), D_MODEL),
        "w_xv": w(ks[18], (L, D_MODEL, D_MODEL), D_MODEL),
        "g_qnorm": gain(ks[19], (L, X_HEAD_DIM)),
        "g_knorm": gain(ks[20], (L, X_HEAD_DIM)),
        "w_xo": w(ks[21], (L, D_MODEL, D_MODEL), D_MODEL),
        "g_ff2": gain(ks[22], (L, D_MODEL)),
        "w_ff2_gate": w(ks[23], (L, D_MODEL, D_FF), D_MODEL),
        "w_ff2_up": w(ks[24], (L, D_MODEL, D_FF), D_MODEL),
        "w_ff2_down": w(ks[25], (L, D_FF, D_MODEL), D_FF),
    }


def reference(x, mem, g_ff1, w_ff1_gate, w_ff1_up, w_ff1_down, g_mix, w_in, b_gate,
              w_conv, b_conv, g_mlstm_head, w_out, g_xattn, g_mem, w_xq, w_xk, w_xv,
              g_qnorm, g_knorm, w_xo, g_ff2, w_ff2_gate, w_ff2_up, w_ff2_down):
    for l in range(DEPTH):
        x = hybrid_layer(x, mem, g_ff1[l], w_ff1_gate[l], w_ff1_up[l], w_ff1_down[l],
                         g_mix[l], w_in[l], b_gate[l], w_conv[l], b_conv[l],
                         g_mlstm_head[l], w_out[l], g_xattn[l], g_mem[l], w_xq[l],
                         w_xk[l], w_xv[l], g_qnorm[l], g_knorm[l], w_xo[l], g_ff2[l],
                         w_ff2_gate[l], w_ff2_up[l], w_ff2_down[l])
    return x
```

```python
import functools

import jax
import jax.numpy as jnp
from jax import lax
from jax.experimental import pallas as pl
from jax.experimental.pallas import tpu as pltpu

F32 = jnp.float32
BF16 = jnp.bfloat16

EPS = 1e-6
SB_HEADS = 8
SB_HEAD_DIM = 64
SB_WIDTH = SB_HEADS * SB_HEAD_DIM
ML_HEADS = 4
ML_HEAD_DIM = 128
ML_WIDTH = ML_HEADS * ML_HEAD_DIM
ML_CHUNK = 64
CONV_WIDTH = 4
X_HEADS = 4

V7X_VMEM_BYTES = 64 * 1024 * 1024
VMEM_LIMIT_BYTES = V7X_VMEM_BYTES - 8 * 1024 * 1024
LANES = 128

FFN_TM = 1024
FFN_TF = 256
PROJ_TM = 512
POST_TM = 512
SB_TQ = 256
SB_TK = 256


def _params(semantics):
    return pltpu.CompilerParams(dimension_semantics=semantics,
                                vmem_limit_bytes=VMEM_LIMIT_BYTES)


def _rms(x, g):
    return x * lax.rsqrt(jnp.mean(x * x, axis=-1, keepdims=True) + EPS) * g


def _dot(a, b):
    return jnp.dot(a, b, preferred_element_type=F32)


def _dot_nt(a, b):
    return lax.dot_general(a, b, (((1,), (1,)), ((), ())), preferred_element_type=F32)


def _dot_tn(a, b):
    return lax.dot_general(a, b, (((0,), (0,)), ((), ())), preferred_element_type=F32)


def _log_sigmoid(x):
    return -(jnp.maximum(-x, 0.0) + jnp.log1p(jnp.exp(-jnp.abs(x))))


def _ffn_kernel(x_ref, g_ref, wg_ref, wu_ref, wd_ref, o_ref, xn_ref, acc_ref):
    j = pl.program_id(1)

    @pl.when(j == 0)
    def _():
        xn_ref[...] = _rms(x_ref[...], g_ref[...]).astype(BF16)
        acc_ref[...] = jnp.zeros_like(acc_ref)

    xn = xn_ref[...]
    gate = _dot(xn, wg_ref[...])
    up = _dot(xn, wu_ref[...])
    h = gate * jax.nn.sigmoid(gate) * up
    acc_ref[...] += _dot(h.astype(BF16), wd_ref[...])

    @pl.when(j == pl.num_programs(1) - 1)
    def _():
        o_ref[...] = x_ref[...] + 0.5 * acc_ref[...]


def _ffn(x, g, wg, wu, wd):
    t, d = x.shape
    f = wg.shape[1]
    tm, tf = min(FFN_TM, t), min(FFN_TF, f)
    return pl.pallas_call(
        _ffn_kernel,
        grid=(t // tm, f // tf),
        in_specs=[
            pl.BlockSpec((tm, d), lambda i, j: (i, 0)),
            pl.BlockSpec((1, d), lambda i, j: (0, 0)),
            pl.BlockSpec((d, tf), lambda i, j: (0, j)),
            pl.BlockSpec((d, tf), lambda i, j: (0, j)),
            pl.BlockSpec((tf, d), lambda i, j: (j, 0)),
        ],
        out_specs=pl.BlockSpec((tm, d), lambda i, j: (i, 0)),
        out_shape=jax.ShapeDtypeStruct((t, d), F32),
        scratch_shapes=[pltpu.VMEM((tm, d), BF16), pltpu.VMEM((tm, d), F32)],
        compiler_params=_params(("parallel", "arbitrary")),
        name="ffn",
    )(x, g.reshape(1, d), wg, wu, wd)


def _proj_kernel(x_ref, g_ref, wq_ref, wkt_ref, wv_ref, wml_ref, wgate_ref,
                 sbq_ref, sbkt_ref, sbv_ref, mlq_ref, mlk_ref, mlv_ref, mlo_ref, gates_ref):
    u = _rms(x_ref[...], g_ref[...]).astype(BF16)
    sbq_ref[...] = (_dot(u, wq_ref[...]) * (SB_HEAD_DIM ** -0.5)).astype(BF16)
    sbkt_ref[...] = _dot_nt(wkt_ref[...], u).astype(BF16)
    sbv_ref[...] = _dot(u, wv_ref[...]).astype(BF16)
    w = ML_WIDTH
    mlq_ref[...] = _dot(u, wml_ref[:, 0 * w:1 * w])
    mlk_ref[...] = _dot(u, wml_ref[:, 1 * w:2 * w])
    mlv_ref[...] = _dot(u, wml_ref[:, 2 * w:3 * w]).astype(BF16)
    mlo_ref[...] = _dot(u, wml_ref[:, 3 * w:4 * w])
    gates_ref[...] = _dot(u, wgate_ref[...])


def _proj(x, g, wq, wkt, wv, wml, wgate):
    t, d = x.shape
    tm = min(PROJ_TM, t)
    row = lambda i: (i, 0)
    fixed = lambda i: (0, 0)
    return pl.pallas_call(
        _proj_kernel,
        grid=(t // tm,),
        in_specs=[
            pl.BlockSpec((tm, d), row),
            pl.BlockSpec((1, d), fixed),
            pl.BlockSpec(wq.shape, fixed),
            pl.BlockSpec(wkt.shape, fixed),
            pl.BlockSpec(wv.shape, fixed),
            pl.BlockSpec(wml.shape, fixed),
            pl.BlockSpec(wgate.shape, fixed),
        ],
        out_specs=[
            pl.BlockSpec((tm, SB_WIDTH), row),
            pl.BlockSpec((SB_WIDTH, tm), lambda i: (0, i)),
            pl.BlockSpec((tm, SB_WIDTH), row),
            pl.BlockSpec((tm, ML_WIDTH), row),
            pl.BlockSpec((tm, ML_WIDTH), row),
            pl.BlockSpec((tm, ML_WIDTH), row),
            pl.BlockSpec((tm, ML_WIDTH), row),
            pl.BlockSpec((tm, LANES), row),
        ],
        out_shape=[
            jax.ShapeDtypeStruct((t, SB_WIDTH), BF16),
            jax.ShapeDtypeStruct((SB_WIDTH, t), BF16),
            jax.ShapeDtypeStruct((t, SB_WIDTH), BF16),
            jax.ShapeDtypeStruct((t, ML_WIDTH), F32),
            jax.ShapeDtypeStruct((t, ML_WIDTH), F32),
            jax.ShapeDtypeStruct((t, ML_WIDTH), BF16),
            jax.ShapeDtypeStruct((t, ML_WIDTH), F32),
            jax.ShapeDtypeStruct((t, LANES), F32),
        ],
        compiler_params=_params(("parallel",)),
        name="proj",
    )(x, g.reshape(1, d), wq, wkt, wv, wml, wgate)


def _sb_kernel(q_ref, kt_ref, v_ref, o_ref, acc_ref, *, tq, tk):
    i = pl.program_id(2)
    half = SB_HEAD_DIM
    q = q_ref[...]
    lane = lax.broadcasted_iota(jnp.int32, q.shape, 1)
    zero = jnp.zeros_like(q)
    q_heads = (jnp.where(lane < half, q, zero), jnp.where(lane >= half, q, zero))

    r = lax.broadcasted_iota(jnp.int32, (tk, tk), 0)
    c = lax.broadcasted_iota(jnp.int32, (tk, tk), 1)
    suffix_ones = jnp.where(r > c, 1.0, 0.0).astype(BF16)
    tr = lax.broadcasted_iota(jnp.int32, (tq, tk), 0)
    tc = lax.broadcasted_iota(jnp.int32, (tq, tk), 1)
    strict = tc < tr

    acc_ref[...] = jnp.zeros_like(acc_ref)

    def block(kb, carries, masked):
        start = pl.multiple_of(kb * tk, tk)
        kt = kt_ref[:, pl.ds(start, tk)]
        v = v_ref[pl.ds(start, tk), :]
        new_carries = []
        for h in range(2):
            z = _dot(q_heads[h], kt)
            log_not = -(jnp.maximum(z, 0.0) + jnp.log1p(jnp.exp(-jnp.abs(z))))
            log_beta = log_not + z
            if masked:
                log_not = jnp.where(strict, log_not, 0.0)
            hi = log_not.astype(BF16)
            lo = (log_not - hi.astype(F32)).astype(BF16)
            sums = _dot(jnp.concatenate([hi, lo], axis=0), suffix_ones)
            later = sums[:tq] + sums[tq:]
            a = jnp.exp(log_beta + later + carries[h])
            if masked:
                a = jnp.where(strict, a, 0.0)
            acc_ref[h] += _dot(a.astype(BF16), v)
            new_carries.append(carries[h] + later[:, 0:1] + log_not[:, 0:1])
        return tuple(new_carries)

    zeros = jnp.zeros((tq, 1), F32)
    carries = block(i, (zeros, zeros), True)

    def body(n, carries):
        return block(i - 1 - n, carries, False)

    lax.fori_loop(0, i, body, carries)
    olane = lax.broadcasted_iota(jnp.int32, (tq, LANES), 1)
    o_ref[...] = jnp.where(olane < half, acc_ref[0], acc_ref[1]).astype(o_ref.dtype)


def _sb_attention(sbq, sbkt, sbv, batch, seq):
    t = sbq.shape[0]
    tq, tk = min(SB_TQ, seq), min(SB_TK, seq)
    assert tq == tk
    nq = seq // tq
    pairs = SB_WIDTH // LANES
    return pl.pallas_call(
        functools.partial(_sb_kernel, tq=tq, tk=tk),
        grid=(batch, pairs, nq),
        in_specs=[
            pl.BlockSpec((tq, LANES), lambda b, p, i: (b * nq + i, p)),
            pl.BlockSpec((LANES, seq), lambda b, p, i: (p, b)),
            pl.BlockSpec((seq, LANES), lambda b, p, i: (b, p)),
        ],
        out_specs=pl.BlockSpec((tq, LANES), lambda b, p, i: (b * nq + i, p)),
        out_shape=jax.ShapeDtypeStruct((t, SB_WIDTH), BF16),
        scratch_shapes=[pltpu.VMEM((2, tq, LANES), F32)],
        compiler_params=_params(("parallel", "parallel", "arbitrary")),
        name="sb_attention",
    )(sbq, sbkt, sbv)


def _mlstm_kernel(q_ref, k_ref, v_ref, og_ref, grow_ref, gcol_ref, brow_ref, bcol_ref,
                  wconv_ref, bconv_ref, ghead_ref, o_ref, c_ref, n_ref, *, n_chunks):
    L, d, H = ML_CHUNK, ML_HEAD_DIM, ML_HEADS
    c_ref[...] = jnp.zeros_like(c_ref)
    n_ref[...] = jnp.zeros_like(n_ref)
    rr = lax.broadcasted_iota(jnp.int32, (L, L), 0)
    cc = lax.broadcasted_iota(jnp.int32, (L, L), 1)
    causal = cc <= rr
    causal_t = rr <= cc

    def conv_silu(ref, col0, r0, first, wofs):
        cols = pl.ds(col0, d)
        cur = ref[pl.ds(r0, L), cols]
        prev = ref[pl.ds(pl.multiple_of(jnp.maximum(r0 - 8, 0), 8), 8), cols]
        prev = jnp.where(first, 0.0, prev)
        ext = jnp.concatenate([prev, cur], axis=0)
        y = bconv_ref[:, pl.ds(wofs + col0, d)]
        for j in range(CONV_WIDTH):
            s0 = 8 - (CONV_WIDTH - 1) + j
            y = y + ext[s0:s0 + L] * wconv_ref[j:j + 1, pl.ds(wofs + col0, d)]
        return y * jax.nn.sigmoid(y)

    def chunk(ci, ms):
        r0 = pl.multiple_of(ci * L, L)
        first = ci == 0
        g_row = grow_ref[ci] + brow_ref[...]
        g_col = gcol_ref[ci] + bcol_ref[...]
        lf_row = _log_sigmoid(g_row)
        lf_col = _log_sigmoid(g_col)
        new_ms = []
        for h in range(H):
            col0 = h * d
            li_r = g_row[h:h + 1, :]
            li_c = g_col[:, h:h + 1]
            lf_r = lf_row[H + h:H + h + 1, :]
            lf_c = lf_col[:, H + h:H + h + 1]
            bcum_c = jnp.sum(jnp.where(causal, lf_r, 0.0), axis=1, keepdims=True)
            bcum_r = jnp.sum(jnp.where(causal_t, lf_c, 0.0), axis=0, keepdims=True)
            btot = bcum_c[L - 1:L, :]
            m_prev = ms[h]
            dmat = jnp.where(causal, bcum_c - bcum_r + li_r, -jnp.inf)
            inter = bcum_c + m_prev
            m_t = jnp.maximum(jnp.max(dmat, axis=1, keepdims=True), inter)
            w_intra = jnp.exp(dmat - m_t)
            w_inter = jnp.exp(inter - m_t)

            qf = conv_silu(q_ref, col0, r0, first, 0) * (d ** -0.5)
            kf = conv_silu(k_ref, col0, r0, first, ML_WIDTH)
            vb = v_ref[pl.ds(r0, L), pl.ds(col0, d)]
            qb = qf.astype(BF16)
            kb = kf.astype(BF16)
            c_st = c_ref[h]
            n_st = n_ref[h]

            sc = _dot_nt(qb, kb) * w_intra
            num = _dot(sc.astype(BF16), vb) + w_inter * _dot(qb, c_st.astype(BF16))
            den = (jnp.sum(sc, axis=1, keepdims=True)
                   + w_inter * jnp.sum(qf * n_st, axis=1, keepdims=True))
            hval = num / jnp.maximum(jnp.abs(den), jnp.exp(-m_t))

            m_new = m_t[L - 1:L, :]
            w_state = jnp.exp(btot - bcum_c + li_c - m_new)
            decay = jnp.exp(btot + m_prev - m_new)
            kw = kf * w_state
            c_ref[h] = decay * c_st + _dot_tn(kw.astype(BF16), vb)
            n_ref[h] = decay * n_st + jnp.sum(kw, axis=0, keepdims=True)
            new_ms.append(m_new)

            hn = _rms(hval, ghead_ref[:, pl.ds(col0, d)])
            og = og_ref[pl.ds(r0, L), pl.ds(col0, d)]
            o_ref[pl.ds(r0, L), pl.ds(col0, d)] = (hn * jax.nn.sigmoid(og)).astype(o_ref.dtype)
        return tuple(new_ms)

    m0 = jnp.zeros((1, 1), F32)
    lax.fori_loop(0, n_chunks, chunk, (m0,) * H)


def _mlstm(mlq, mlk, mlv, mlo, grow, gcol, brow, bcol, wconv, bconv, ghead, batch, seq):
    t = mlq.shape[0]
    n_chunks = seq // ML_CHUNK
    seq_blk = pl.BlockSpec((seq, ML_WIDTH), lambda b: (b, 0))
    fixed2 = lambda b: (0, 0)
    return pl.pallas_call(
        functools.partial(_mlstm_kernel, n_chunks=n_chunks),
        grid=(batch,),
        in_specs=[
            seq_blk, seq_blk, seq_blk, seq_blk,
            pl.BlockSpec((n_chunks, 2 * ML_HEADS, ML_CHUNK), lambda b: (b, 0, 0)),
            pl.BlockSpec((n_chunks, ML_CHUNK, 2 * ML_HEADS), lambda b: (b, 0, 0)),
            pl.BlockSpec(brow.shape, fixed2),
            pl.BlockSpec(bcol.shape, fixed2),
            pl.BlockSpec(wconv.shape, fixed2),
            pl.BlockSpec(bconv.shape, fixed2),
            pl.BlockSpec(ghead.shape, fixed2),
        ],
        out_specs=seq_blk,
        out_shape=jax.ShapeDtypeStruct((t, ML_WIDTH), BF16),
        scratch_shapes=[pltpu.VMEM((ML_HEADS, ML_HEAD_DIM, ML_HEAD_DIM), F32),
                        pltpu.VMEM((ML_HEADS, 1, ML_HEAD_DIM), F32)],
        compiler_params=_params(("parallel",)),
        name="mlstm",
    )(mlq, mlk, mlv, mlo, grow, gcol, brow, bcol, wconv, bconv, ghead)


def _memkv_kernel(mem_ref, gm_ref, wk_ref, wv_ref, gk_ref, k_ref, v_ref):
    mn = _rms(mem_ref[...], gm_ref[...]).astype(BF16)
    k = _dot(mn, wk_ref[...])
    hd = k.shape[1] // X_HEADS
    for h in range(X_HEADS):
        k_ref[:, h * hd:(h + 1) * hd] = _rms(k[:, h * hd:(h + 1) * hd], gk_ref[...]).astype(BF16)
    v_ref[...] = _dot(mn, wv_ref[...]).astype(BF16)


def _memkv(mem2d, gm, wk, wv, gk, batch):
    rows, d = mem2d.shape
    m = rows // batch
    blk = pl.BlockSpec((m, d), lambda b: (b, 0))
    fixed = lambda b: (0, 0)
    return pl.pallas_call(
        _memkv_kernel,
        grid=(batch,),
        in_specs=[blk, pl.BlockSpec((1, d), fixed), pl.BlockSpec(wk.shape, fixed),
                  pl.BlockSpec(wv.shape, fixed), pl.BlockSpec((1, d // X_HEADS), fixed)],
        out_specs=[blk, blk],
        out_shape=[jax.ShapeDtypeStruct((rows, d), BF16)] * 2,
        compiler_params=_params(("parallel",)),
        name="memkv",
    )(mem2d, gm.reshape(1, d), wk, wv, gk.reshape(1, d // X_HEADS))


def _post_kernel(x_ref, sb_ref, ml_ref, wo_ref, gx_ref, wq_ref, gq_ref, k_ref, v_ref, wxo_ref,
                 o_ref):
    d = x_ref.shape[1]
    hd = d // X_HEADS
    x1 = (x_ref[...] + _dot(sb_ref[...], wo_ref[0:SB_WIDTH, :])
          + _dot(ml_ref[...], wo_ref[SB_WIDTH:SB_WIDTH + ML_WIDTH, :]))
    u = _rms(x1, gx_ref[...]).astype(BF16)
    q = _dot(u, wq_ref[...])
    outs = []
    for h in range(X_HEADS):
        cols = slice(h * hd, (h + 1) * hd)
        qn = _rms(q[:, cols], gq_ref[...]).astype(BF16)
        s = _dot_nt(qn, k_ref[:, cols]) * (hd ** -0.5)
        e = jnp.exp(s - jnp.max(s, axis=-1, keepdims=True))
        o = _dot(e.astype(BF16), v_ref[:, cols]) / jnp.sum(e, axis=-1, keepdims=True)
        outs.append(o.astype(BF16))
    o_ref[...] = x1 + _dot(jnp.concatenate(outs, axis=1), wxo_ref[...])


def _post(x, sb, ml, wo, gx, wxq, gq, kmem, vmem, wxo, seq):
    t, d = x.shape
    tm = min(POST_TM, seq)
    per_seq = seq // tm
    m = kmem.shape[0] // (t // seq)
    row = lambda i: (i, 0)
    fixed = lambda i: (0, 0)
    mem_blk = pl.BlockSpec((m, d), lambda i: (i // per_seq, 0))
    return pl.pallas_call(
        _post_kernel,
        grid=(t // tm,),
        in_specs=[
            pl.BlockSpec((tm, d), row),
            pl.BlockSpec((tm, SB_WIDTH), row),
            pl.BlockSpec((tm, ML_WIDTH), row),
            pl.BlockSpec(wo.shape, fixed),
            pl.BlockSpec((1, d), fixed),
            pl.BlockSpec(wxq.shape, fixed),
            pl.BlockSpec((1, d // X_HEADS), fixed),
            mem_blk, mem_blk,
            pl.BlockSpec(wxo.shape, fixed),
        ],
        out_specs=pl.BlockSpec((tm, d), row),
        out_shape=jax.ShapeDtypeStruct((t, d), F32),
        compiler_params=_params(("parallel",)),
        name="post",
    )(x, sb, ml, wo, gx.reshape(1, d), wxq, gq.reshape(1, d // X_HEADS), kmem, vmem, wxo)


def _layer(x, mem2d, batch, seq, g_ff1, w_ff1_gate, w_ff1_up, w_ff1_down, g_mix, w_in, b_gate,
           w_conv, b_conv, g_mlstm_head, w_out, g_xattn, g_mem, w_xq, w_xk, w_xv, g_qnorm,
           g_knorm, w_xo, g_ff2, w_ff2_gate, w_ff2_up, w_ff2_down):
    bf = lambda w: w.astype(BF16)
    d = x.shape[1]
    x = _ffn(x, g_ff1, bf(w_ff1_gate), bf(w_ff1_up), bf(w_ff1_down))

    s, w = SB_WIDTH, ML_WIDTH
    n_gate = 2 * ML_HEADS
    w_gates = jnp.pad(w_in[:, 3 * s + 4 * w:], ((0, 0), (0, LANES - n_gate)))
    sbq, sbkt, sbv, mlq, mlk, mlv, mlo, gates = _proj(
        x, g_mix, bf(w_in[:, 0:s]), bf(w_in[:, s:2 * s].T), bf(w_in[:, 2 * s:3 * s]),
        bf(w_in[:, 3 * s:3 * s + 4 * w]), bf(w_gates))

    sb = _sb_attention(sbq, sbkt, sbv, batch, seq)

    n_chunks = seq // ML_CHUNK
    g8 = gates[:, :n_gate].reshape(batch * n_chunks, ML_CHUNK, n_gate)
    ml = _mlstm(mlq, mlk, mlv, mlo, g8.transpose(0, 2, 1), g8,
                b_gate.reshape(n_gate, 1), b_gate.reshape(1, n_gate),
                w_conv, b_conv.reshape(1, 2 * w), g_mlstm_head.reshape(1, w), batch, seq)

    kmem, vmem = _memkv(mem2d, g_mem, bf(w_xk), bf(w_xv), g_knorm, batch)
    x = _post(x, sb, ml, bf(w_out), g_xattn, bf(w_xq), g_qnorm, kmem, vmem, bf(w_xo), seq)
    x = _ffn(x, g_ff2, bf(w_ff2_gate), bf(w_ff2_up), bf(w_ff2_down))
    return x


def kernel(x, mem, g_ff1, w_ff1_gate, w_ff1_up, w_ff1_down, g_mix, w_in, b_gate, w_conv, b_conv, g_mlstm_head, w_out, g_xattn, g_mem, w_xq, w_xk, w_xv, g_qnorm, g_knorm, w_xo, g_ff2, w_ff2_gate, w_ff2_up, w_ff2_down):
    batch, seq, d = x.shape
    depth = g_ff1.shape[0]
    h = x.reshape(batch * seq, d)
    mem2d = mem.reshape(batch * mem.shape[1], d)
    per_layer = (g_ff1, w_ff1_gate, w_ff1_up, w_ff1_down, g_mix, w_in, b_gate, w_conv, b_conv,
                 g_mlstm_head, w_out, g_xattn, g_mem, w_xq, w_xk, w_xv, g_qnorm, g_knorm, w_xo,
                 g_ff2, w_ff2_gate, w_ff2_up, w_ff2_down)
    for l in range(depth):
        h = _layer(h, mem2d, batch, seq, *(p[l] for p in per_layer))
    return h.reshape(batch, seq, d)
```

```python
import functools

import jax
import jax.numpy as jnp
from jax import lax
from jax.experimental import pallas as pl
from jax.experimental.pallas import tpu as pltpu

F32 = jnp.float32
BF16 = jnp.bfloat16

EPS = 1e-6
LOG2E = 1.4426950408889634
SB_HEADS = 8
SB_HEAD_DIM = 64
SB_WIDTH = SB_HEADS * SB_HEAD_DIM
ML_HEADS = 4
ML_HEAD_DIM = 128
ML_WIDTH = ML_HEADS * ML_HEAD_DIM
ML_CHUNK = 64
CONV_WIDTH = 4
X_HEADS = 4

V7X_VMEM_BYTES = 64 * 1024 * 1024
VMEM_LIMIT_BYTES = V7X_VMEM_BYTES - 8 * 1024 * 1024
LANES = 128

FFN_TM = 1024
FFN_TF = 256
PROJ_TM = 512
POST_TM = 512
SB_TQ = 256
SB_TK = 256
SB_PAIRS_PER_STEP = 4
SB_STAGE_SKEW = 1


def _params(semantics):
    return pltpu.CompilerParams(dimension_semantics=semantics,
                                vmem_limit_bytes=VMEM_LIMIT_BYTES)


def _rms(x, g):
    return x * lax.rsqrt(jnp.mean(x * x, axis=-1, keepdims=True) + EPS) * g


def _dot(a, b):
    return jnp.dot(a, b, preferred_element_type=F32)


def _dot_nt(a, b):
    return lax.dot_general(a, b, (((1,), (1,)), ((), ())), preferred_element_type=F32)


def _dot_tn(a, b):
    return lax.dot_general(a, b, (((0,), (0,)), ((), ())), preferred_element_type=F32)


def _log_sigmoid(x):
    return -(jnp.maximum(-x, 0.0) + jnp.log1p(jnp.exp(-jnp.abs(x))))


def _ffn_kernel(x_ref, g_ref, wg_ref, wu_ref, wd_ref, o_ref, xn_ref, acc_ref):
    j = pl.program_id(1)

    @pl.when(j == 0)
    def _():
        xn_ref[...] = _rms(x_ref[...], g_ref[...]).astype(BF16)
        acc_ref[...] = jnp.zeros_like(acc_ref)

    xn = xn_ref[...]
    gate = _dot(xn, wg_ref[...])
    up = _dot(xn, wu_ref[...])
    h = gate * jax.nn.sigmoid(gate) * up
    acc_ref[...] += _dot(h.astype(BF16), wd_ref[...])

    @pl.when(j == pl.num_programs(1) - 1)
    def _():
        o_ref[...] = x_ref[...] + 0.5 * acc_ref[...]


def _ffn(x, g, wg, wu, wd):
    t, d = x.shape
    f = wg.shape[1]
    tm, tf = min(FFN_TM, t), min(FFN_TF, f)
    return pl.pallas_call(
        _ffn_kernel,
        grid=(t // tm, f // tf),
        in_specs=[
            pl.BlockSpec((tm, d), lambda i, j: (i, 0)),
            pl.BlockSpec((1, d), lambda i, j: (0, 0)),
            pl.BlockSpec((d, tf), lambda i, j: (0, j)),
            pl.BlockSpec((d, tf), lambda i, j: (0, j)),
            pl.BlockSpec((tf, d), lambda i, j: (j, 0)),
        ],
        out_specs=pl.BlockSpec((tm, d), lambda i, j: (i, 0)),
        out_shape=jax.ShapeDtypeStruct((t, d), F32),
        scratch_shapes=[pltpu.VMEM((tm, d), BF16), pltpu.VMEM((tm, d), F32)],
        compiler_params=_params(("parallel", "arbitrary")),
        name="ffn",
    )(x, g.reshape(1, d), wg, wu, wd)


def _proj_kernel(x_ref, g_ref, wq_ref, wkt_ref, wv_ref, wml_ref, wgate_ref,
                 sbq_ref, sbkt_ref, sbv_ref, mlq_ref, mlk_ref, mlv_ref, mlo_ref, gates_ref):
    u = _rms(x_ref[...], g_ref[...]).astype(BF16)
    sbq_ref[...] = (_dot(u, wq_ref[...]) * (LOG2E * SB_HEAD_DIM ** -0.5)).astype(BF16)
    sbkt_ref[...] = _dot_nt(wkt_ref[...], u).astype(BF16)
    sbv_ref[...] = _dot(u, wv_ref[...]).astype(BF16)
    w = ML_WIDTH
    mlq_ref[...] = _dot(u, wml_ref[:, 0 * w:1 * w])
    mlk_ref[...] = _dot(u, wml_ref[:, 1 * w:2 * w])
    mlv_ref[...] = _dot(u, wml_ref[:, 2 * w:3 * w]).astype(BF16)
    mlo_ref[...] = _dot(u, wml_ref[:, 3 * w:4 * w])
    gates_ref[...] = _dot(u, wgate_ref[...])


def _proj(x, g, wq, wkt, wv, wml, wgate):
    t, d = x.shape
    tm = min(PROJ_TM, t)
    row = lambda i: (i, 0)
    fixed = lambda i: (0, 0)
    return pl.pallas_call(
        _proj_kernel,
        grid=(t // tm,),
        in_specs=[
            pl.BlockSpec((tm, d), row),
            pl.BlockSpec((1, d), fixed),
            pl.BlockSpec(wq.shape, fixed),
            pl.BlockSpec(wkt.shape, fixed),
            pl.BlockSpec(wv.shape, fixed),
            pl.BlockSpec(wml.shape, fixed),
            pl.BlockSpec(wgate.shape, fixed),
        ],
        out_specs=[
            pl.BlockSpec((tm, SB_WIDTH), row),
            pl.BlockSpec((SB_WIDTH, tm), lambda i: (0, i)),
            pl.BlockSpec((tm, SB_WIDTH), row),
            pl.BlockSpec((tm, ML_WIDTH), row),
            pl.BlockSpec((tm, ML_WIDTH), row),
            pl.BlockSpec((tm, ML_WIDTH), row),
            pl.BlockSpec((tm, ML_WIDTH), row),
            pl.BlockSpec((tm, LANES), row),
        ],
        out_shape=[
            jax.ShapeDtypeStruct((t, SB_WIDTH), BF16),
            jax.ShapeDtypeStruct((SB_WIDTH, t), BF16),
            jax.ShapeDtypeStruct((t, SB_WIDTH), BF16),
            jax.ShapeDtypeStruct((t, ML_WIDTH), F32),
            jax.ShapeDtypeStruct((t, ML_WIDTH), F32),
            jax.ShapeDtypeStruct((t, ML_WIDTH), BF16),
            jax.ShapeDtypeStruct((t, ML_WIDTH), F32),
            jax.ShapeDtypeStruct((t, LANES), F32),
        ],
        compiler_params=_params(("parallel",)),
        name="proj",
    )(x, g.reshape(1, d), wq, wkt, wv, wml, wgate)


def _sb_kernel(q_ref, kt_ref, v_ref, o_ref, acc_ref, *, tq, tk, pairs):
    i = pl.program_id(2)
    half = SB_HEAD_DIM
    n_heads = 2 * pairs
    lane = lax.broadcasted_iota(jnp.int32, (tq, LANES), 1)
    q_heads = []
    for p in range(pairs):
        q = q_ref[:, p * LANES:(p + 1) * LANES]
        zero = jnp.zeros_like(q)
        q_heads += [jnp.where(lane < half, q, zero), jnp.where(lane >= half, q, zero)]

    r = lax.broadcasted_iota(jnp.int32, (2 * tk, tk), 0)
    c = lax.broadcasted_iota(jnp.int32, (2 * tk, tk), 1)
    suffix_ones = jnp.where((r & (tk - 1)) > c, 1.0, 0.0).astype(BF16)
    tr =lax.broadcasted_iota(jnp.int32, (tq, tk), 0)
    tc = lax.broadcasted_iota(jnp.int32, (tq, tk), 1)
    strict = tc < tr

    acc_ref[...] = jnp.zeros_like(acc_ref)

    def block(kb, carries, masked):
        start = pl.multiple_of(kb * tk, tk)
        pair = lambda h: slice((h // 2) * LANES, (h // 2 + 1) * LANES)
        z2, nl, later = {}, {}, {}
        new_carries = [None] * n_heads

        def scores(h):
            z2[h] = _dot(q_heads[h], kt_ref[pair(h), pl.ds(start, tk)])

        def suffix_sums(h):
            z = z2[h]
            neg_abs = lax.bitcast_convert_type(
                lax.bitcast_convert_type(z, jnp.uint32) | jnp.uint32(0x80000000), F32)
            x = jnp.maximum(z, 0.0) + jnp.log(1.0 + jnp.exp2(neg_abs)) * LOG2E
            if masked:
                x = jnp.where(strict, x, 0.0)
            hi = x.astype(BF16)
            lo = (x - hi.astype(F32)).astype(BF16)
            nl[h] = x
            later[h] = _dot(jnp.concatenate([hi, lo], axis=1), suffix_ones)

        def weights_times_values(h):
            a = jnp.exp2(z2.pop(h) - (nl[h] + later[h] + carries[h]))
            if masked:
                a = jnp.where(strict, a, 0.0)
            acc_ref[h] += _dot(a.astype(BF16), v_ref[pl.ds(start, tk), pair(h)])
            new_carries[h] = carries[h] + later.pop(h)[:, 0:1] + nl.pop(h)[:, 0:1]

        for step in range(n_heads + 2 * SB_STAGE_SKEW):
            if step < n_heads:
                scores(step)
            if 0 <= step - SB_STAGE_SKEW < n_heads:
                suffix_sums(step - SB_STAGE_SKEW)
            if 0 <= step - 2 * SB_STAGE_SKEW < n_heads:
                weights_times_values(step - 2 * SB_STAGE_SKEW)
        return tuple(new_carries)

    zeros = jnp.zeros((tq, 1), F32)
    carries = block(i, (zeros,) * n_heads, True)

    def body(n, carries):
        return block(i - 1 - n, carries, False)

    lax.fori_loop(0, i, body, carries)
    for p in range(pairs):
        o_ref[:, p * LANES:(p + 1) * LANES] = jnp.where(
            lane < half, acc_ref[2 * p], acc_ref[2 * p + 1]).astype(o_ref.dtype)


def _sb_attention(sbq, sbkt, sbv, batch, seq):
    t = sbq.shape[0]
    tq, tk = min(SB_TQ, seq), min(SB_TK, seq)
    assert tq == tk
    nq = seq // tq
    pairs = SB_PAIRS_PER_STEP
    width = pairs * LANES
    groups = SB_WIDTH // width
    return pl.pallas_call(
        functools.partial(_sb_kernel, tq=tq, tk=tk, pairs=pairs),
        grid=(batch, groups, nq),
        in_specs=[
            pl.BlockSpec((tq, width), lambda b, p, i: (b * nq + i, p)),
            pl.BlockSpec((width, seq), lambda b, p, i: (p, b)),
            pl.BlockSpec((seq, width), lambda b, p, i: (b, p)),
        ],
        out_specs=pl.BlockSpec((tq, width), lambda b, p, i: (b * nq + i, p)),
        out_shape=jax.ShapeDtypeStruct((t, SB_WIDTH), BF16),
        scratch_shapes=[pltpu.VMEM((2 * pairs, tq, LANES), F32)],
        compiler_params=_params(("parallel", "parallel", "arbitrary")),
        name="sb_attention",
    )(sbq, sbkt, sbv)


def _mlstm_kernel(q_ref, k_ref, v_ref, og_ref, grow_ref, gcol_ref, brow_ref, bcol_ref,
                  wconv_ref, bconv_ref, ghead_ref, o_ref, c_ref, n_ref, *, n_chunks):
    L, d, H = ML_CHUNK, ML_HEAD_DIM, ML_HEADS
    c_ref[...] = jnp.zeros_like(c_ref)
    n_ref[...] = jnp.zeros_like(n_ref)
    rr = lax.broadcasted_iota(jnp.int32, (L, L), 0)
    cc = lax.broadcasted_iota(jnp.int32, (L, L), 1)
    causal = cc <= rr
    causal_t = rr <= cc

    def conv_silu(ref, col0, r0, first, wofs):
        cols = pl.ds(col0, d)
        cur = ref[pl.ds(r0, L), cols]
        prev = ref[pl.ds(pl.multiple_of(jnp.maximum(r0 - 8, 0), 8), 8), cols]
        prev = jnp.where(first, 0.0, prev)
        ext = jnp.concatenate([prev, cur], axis=0)
        y = bconv_ref[:, pl.ds(wofs + col0, d)]
        for j in range(CONV_WIDTH):
            s0 = 8 - (CONV_WIDTH - 1) + j
            y = y + ext[s0:s0 + L] * wconv_ref[j:j + 1, pl.ds(wofs + col0, d)]
        return y * jax.nn.sigmoid(y)

    def chunk(ci, ms):
        r0 = pl.multiple_of(ci * L, L)
        first = ci == 0
        g_row = grow_ref[ci] + brow_ref[...]
        g_col = gcol_ref[ci] + bcol_ref[...]
        lf_row = _log_sigmoid(g_row)
        lf_col = _log_sigmoid(g_col)
        new_ms = []
        for h in range(H):
            col0 = h * d
            li_r = g_row[h:h + 1, :]
            li_c = g_col[:, h:h + 1]
            lf_r = lf_row[H + h:H + h + 1, :]
            lf_c = lf_col[:, H + h:H + h + 1]
            bcum_c = jnp.sum(jnp.where(causal, lf_r, 0.0), axis=1, keepdims=True)
            bcum_r = jnp.sum(jnp.where(causal_t, lf_c, 0.0), axis=0, keepdims=True)
            btot = bcum_c[L - 1:L, :]
            m_prev = ms[h]
            dmat = jnp.where(causal, bcum_c - bcum_r + li_r, -jnp.inf)
            inter = bcum_c + m_prev
            m_t = jnp.maximum(jnp.max(dmat, axis=1, keepdims=True), inter)
            w_intra = jnp.exp(dmat - m_t)
            w_inter = jnp.exp(inter - m_t)

            qf = conv_silu(q_ref, col0, r0, first, 0) * (d ** -0.5)
            kf = conv_silu(k_ref, col0, r0, first, ML_WIDTH)
            vb = v_ref[pl.ds(r0, L), pl.ds(col0, d)]
            qb = qf.astype(BF16)
            kb = kf.astype(BF16)
            c_st = c_ref[h]
            n_st = n_ref[h]

            sc = _dot_nt(qb, kb) * w_intra
            num = _dot(sc.astype(BF16), vb) + w_inter * _dot(qb, c_st.astype(BF16))
            den = (jnp.sum(sc, axis=1, keepdims=True)
                   + w_inter * jnp.sum(qf * n_st, axis=1, keepdims=True))
            hval = num / jnp.maximum(jnp.abs(den), jnp.exp(-m_t))

            m_new = m_t[L - 1:L, :]
            w_state = jnp.exp(btot - bcum_c + li_c - m_new)
            decay = jnp.exp(btot + m_prev - m_new)
            kw = kf * w_state
            c_ref[h] = decay * c_st + _dot_tn(kw.astype(BF16), vb)
            n_ref[h] = decay * n_st + jnp.sum(kw, axis=0, keepdims=True)
            new_ms.append(m_new)

            hn = _rms(hval, ghead_ref[:, pl.ds(col0, d)])
            og = og_ref[pl.ds(r0, L), pl.ds(col0, d)]
            o_ref[pl.ds(r0, L), pl.ds(col0, d)] = (hn * jax.nn.sigmoid(og)).astype(o_ref.dtype)
        return tuple(new_ms)

    m0 = jnp.zeros((1, 1), F32)
    lax.fori_loop(0, n_chunks, chunk, (m0,) * H)


def _mlstm(mlq, mlk, mlv, mlo, grow, gcol, brow, bcol, wconv, bconv, ghead, batch, seq):
    t = mlq.shape[0]
    n_chunks = seq // ML_CHUNK
    seq_blk = pl.BlockSpec((seq, ML_WIDTH), lambda b: (b, 0))
    fixed2 = lambda b: (0, 0)
    return pl.pallas_call(
        functools.partial(_mlstm_kernel, n_chunks=n_chunks),
        grid=(batch,),
        in_specs=[
            seq_blk, seq_blk, seq_blk, seq_blk,
            pl.BlockSpec((n_chunks, 2 * ML_HEADS, ML_CHUNK), lambda b: (b, 0, 0)),
            pl.BlockSpec((n_chunks, ML_CHUNK, 2 * ML_HEADS), lambda b: (b, 0, 0)),
            pl.BlockSpec(brow.shape, fixed2),
            pl.BlockSpec(bcol.shape, fixed2),
            pl.BlockSpec(wconv.shape, fixed2),
            pl.BlockSpec(bconv.shape, fixed2),
            pl.BlockSpec(ghead.shape, fixed2),
        ],
        out_specs=seq_blk,
        out_shape=jax.ShapeDtypeStruct((t, ML_WIDTH), BF16),
        scratch_shapes=[pltpu.VMEM((ML_HEADS, ML_HEAD_DIM, ML_HEAD_DIM), F32),
                        pltpu.VMEM((ML_HEADS, 1, ML_HEAD_DIM), F32)],
        compiler_params=_params(("parallel",)),
        name="mlstm",
    )(mlq, mlk, mlv, mlo, grow, gcol, brow, bcol, wconv, bconv, ghead)


def _memkv_kernel(mem_ref, gm_ref, wk_ref, wv_ref, gk_ref, k_ref, v_ref):
    mn = _rms(mem_ref[...], gm_ref[...]).astype(BF16)
    k = _dot(mn, wk_ref[...])
    hd = k.shape[1] // X_HEADS
    for h in range(X_HEADS):
        k_ref[:, h * hd:(h + 1) * hd] = _rms(k[:, h * hd:(h + 1) * hd], gk_ref[...]).astype(BF16)
    v_ref[...] = _dot(mn, wv_ref[...]).astype(BF16)


def _memkv(mem2d, gm, wk, wv, gk, batch):
    rows, d = mem2d.shape
    m = rows // batch
    blk = pl.BlockSpec((m, d), lambda b: (b, 0))
    fixed = lambda b: (0, 0)
    return pl.pallas_call(
        _memkv_kernel,
        grid=(batch,),
        in_specs=[blk, pl.BlockSpec((1, d), fixed), pl.BlockSpec(wk.shape, fixed),
                  pl.BlockSpec(wv.shape, fixed), pl.BlockSpec((1, d // X_HEADS), fixed)],
        out_specs=[blk, blk],
        out_shape=[jax.ShapeDtypeStruct((rows, d), BF16)] * 2,
        compiler_params=_params(("parallel",)),
        name="memkv",
    )(mem2d, gm.reshape(1, d), wk, wv, gk.reshape(1, d // X_HEADS))


def _post_kernel(x_ref, sb_ref, ml_ref, wo_ref, gx_ref, wq_ref, gq_ref, k_ref, v_ref, wxo_ref,
                 o_ref):
    d = x_ref.shape[1]
    hd = d // X_HEADS
    x1 = (x_ref[...] + _dot(sb_ref[...], wo_ref[0:SB_WIDTH, :])
          + _dot(ml_ref[...], wo_ref[SB_WIDTH:SB_WIDTH + ML_WIDTH, :]))
    u = _rms(x1, gx_ref[...]).astype(BF16)
    q = _dot(u, wq_ref[...])
    outs = []
    for h in range(X_HEADS):
        cols = slice(h * hd, (h + 1) * hd)
        qn = _rms(q[:, cols], gq_ref[...]).astype(BF16)
        s = _dot_nt(qn, k_ref[:, cols]) * (hd ** -0.5)
        e = jnp.exp(s - jnp.max(s, axis=-1, keepdims=True))
        o = _dot(e.astype(BF16), v_ref[:, cols]) / jnp.sum(e, axis=-1, keepdims=True)
        outs.append(o.astype(BF16))
    o_ref[...] = x1 + _dot(jnp.concatenate(outs, axis=1), wxo_ref[...])


def _post(x, sb, ml, wo, gx, wxq, gq, kmem, vmem, wxo, seq):
    t, d = x.shape
    tm = min(POST_TM, seq)
    per_seq = seq // tm
    m = kmem.shape[0] // (t // seq)
    row = lambda i: (i, 0)
    fixed = lambda i: (0, 0)
    mem_blk = pl.BlockSpec((m, d), lambda i: (i // per_seq, 0))
    return pl.pallas_call(
        _post_kernel,
        grid=(t // tm,),
        in_specs=[
            pl.BlockSpec((tm, d), row),
            pl.BlockSpec((tm, SB_WIDTH), row),
            pl.BlockSpec((tm, ML_WIDTH), row),
            pl.BlockSpec(wo.shape, fixed),
            pl.BlockSpec((1, d), fixed),
            pl.BlockSpec(wxq.shape, fixed),
            pl.BlockSpec((1, d // X_HEADS), fixed),
            mem_blk, mem_blk,
            pl.BlockSpec(wxo.shape, fixed),
        ],
        out_specs=pl.BlockSpec((tm, d), row),
        out_shape=jax.ShapeDtypeStruct((t, d), F32),
        compiler_params=_params(("parallel",)),
        name="post",
    )(x, sb, ml, wo, gx.reshape(1, d), wxq, gq.reshape(1, d // X_HEADS), kmem, vmem, wxo)


def _layer(x, mem2d, batch, seq, g_ff1, w_ff1_gate, w_ff1_up, w_ff1_down, g_mix, w_in, b_gate,
           w_conv, b_conv, g_mlstm_head, w_out, g_xattn, g_mem, w_xq, w_xk, w_xv, g_qnorm,
           g_knorm, w_xo, g_ff2, w_ff2_gate, w_ff2_up, w_ff2_down):
    bf = lambda w: w.astype(BF16)
    d = x.shape[1]
    x = _ffn(x, g_ff1, bf(w_ff1_gate), bf(w_ff1_up), bf(w_ff1_down))

    s, w = SB_WIDTH, ML_WIDTH
    n_gate = 2 * ML_HEADS
    w_gates = jnp.pad(w_in[:, 3 * s + 4 * w:], ((0, 0), (0, LANES - n_gate)))
    sbq, sbkt, sbv, mlq, mlk, mlv, mlo, gates = _proj(
        x, g_mix, bf(w_in[:, 0:s]), bf(w_in[:, s:2 * s].T), bf(w_in[:, 2 * s:3 * s]),
        bf(w_in[:, 3 * s:3 * s + 4 * w]), bf(w_gates))

    sb = _sb_attention(sbq, sbkt, sbv, batch, seq)

    n_chunks = seq // ML_CHUNK
    g8 = gates[:, :n_gate].reshape(batch * n_chunks, ML_CHUNK, n_gate)
    ml = _mlstm(mlq, mlk, mlv, mlo, g8.transpose(0, 2, 1), g8,
                b_gate.reshape(n_gate, 1), b_gate.reshape(1, n_gate),
                w_conv, b_conv.reshape(1, 2 * w), g_mlstm_head.reshape(1, w), batch, seq)

    kmem, vmem = _memkv(mem2d, g_mem, bf(w_xk), bf(w_xv), g_knorm, batch)
    x = _post(x, sb, ml, bf(w_out), g_xattn, bf(w_xq), g_qnorm, kmem, vmem, bf(w_xo), seq)
    x = _ffn(x, g_ff2, bf(w_ff2_gate), bf(w_ff2_up), bf(w_ff2_down))
    return x


def kernel(x, mem, g_ff1, w_ff1_gate, w_ff1_up, w_ff1_down, g_mix, w_in, b_gate, w_conv, b_conv, g_mlstm_head, w_out, g_xattn, g_mem, w_xq, w_xk, w_xv, g_qnorm, g_knorm, w_xo, g_ff2, w_ff2_gate, w_ff2_up, w_ff2_down):
    batch, seq, d = x.shape
    depth = g_ff1.shape[0]
    h = x.reshape(batch * seq, d)
    mem2d = mem.reshape(batch * mem.shape[1], d)
    per_layer = (g_ff1, w_ff1_gate, w_ff1_up, w_ff1_down, g_mix, w_in, b_gate, w_conv, b_conv,
                 g_mlstm_head, w_out, g_xattn, g_mem, w_xq, w_xk, w_xv, g_qnorm, g_knorm, w_xo,
                 g_ff2, w_ff2_gate, w_ff2_up, w_ff2_down)
    for l in range(depth):
        h = _layer(h, mem2d, batch, seq, *(p[l] for p in per_layer))
    return h.reshape(batch, seq, d)
```

```python
import functools

import jax
import jax.numpy as jnp
from jax import lax
from jax.experimental import pallas as pl
from jax.experimental.pallas import tpu as pltpu

F32 = jnp.float32
BF16 = jnp.bfloat16

EPS = 1e-6
LOG2E = 1.4426950408889634
SB_HEADS = 8
SB_HEAD_DIM = 64
SB_WIDTH = SB_HEADS * SB_HEAD_DIM
ML_HEADS = 4
ML_HEAD_DIM = 128
ML_WIDTH = ML_HEADS * ML_HEAD_DIM
ML_KERNEL_CHUNK = 128
CONV_WIDTH = 4
X_HEADS = 4

V7X_VMEM_BYTES = 64 * 1024 * 1024
VMEM_LIMIT_BYTES = V7X_VMEM_BYTES - 8 * 1024 * 1024
LANES = 128
SUBLANES = 8

FFN_TM = 512
FFN_TF = 256
PROJ_TM = 512
POST_TM = 512
SB_TQ = 256
SB_TK = 256
SB_PAIRS_PER_STEP = 4
SB_STAGE_SKEW = 1
ML_BATCH_PER_STEP = 2
ML_SEQ_BLOCK = 512


def _params(semantics):
    return pltpu.CompilerParams(dimension_semantics=semantics,
                                vmem_limit_bytes=VMEM_LIMIT_BYTES)


def _rms(x, g):
    return x * lax.rsqrt(jnp.mean(x * x, axis=-1, keepdims=True) + EPS) * g


def _dot(a, b):
    return jnp.dot(a, b, preferred_element_type=F32)


def _dot_nt(a, b):
    return lax.dot_general(a, b, (((1,), (1,)), ((), ())), preferred_element_type=F32)


def _log_sigmoid(x):
    return -(jnp.maximum(-x, 0.0) + jnp.log1p(jnp.exp(-jnp.abs(x))))


def _layer_weight(shape, layer):
    return pl.BlockSpec((None,) + shape, lambda *_: (layer, 0, 0), pipeline_mode=pl.Buffered(1))


def _ffn_kernel(x_ref, g_ref, wg_ref, wu_ref, wd_ref, o_ref, *, tf):
    x = x_ref[...]
    xn = _rms(x, g_ref[...]).astype(BF16)
    hidden = []
    for c in range(wg_ref.shape[1] // tf):
        cols = slice(c * tf, (c + 1) * tf)
        gate = _dot(xn, wg_ref[:, cols])
        up = _dot(xn, wu_ref[:, cols])
        hidden.append((gate * jax.nn.sigmoid(gate) * up).astype(BF16))
    o_ref[...] = x + 0.5 * _dot(jnp.concatenate(hidden, axis=1), wd_ref[...])


def _ffn(x, g, wg, wu, wd, layer):
    t, d = x.shape
    f = wg.shape[2]
    tm, tf = min(FFN_TM, t), min(FFN_TF, f)
    return pl.pallas_call(
        functools.partial(_ffn_kernel, tf=tf),
        grid=(t // tm,),
        in_specs=[
            pl.BlockSpec((tm, d), lambda i: (i, 0)),
            pl.BlockSpec((1, d), lambda i: (0, 0)),
            _layer_weight((d, f), layer), _layer_weight((d, f), layer),
            _layer_weight((f, d), layer),
        ],
        out_specs=pl.BlockSpec((tm, d), lambda i: (i, 0)),
        out_shape=jax.ShapeDtypeStruct((t, d), F32),
        compiler_params=_params(("parallel",)),
        name="ffn",
    )(x, g.reshape(1, d), wg, wu, wd)


def _proj_kernel(x_ref, xprev_ref, g_ref, wq_ref, wkt_ref, wv_ref, wml_ref, wgate_ref,
                 wconv_ref, bconv_ref,
                 sbq_ref, sbkt_ref, sbv_ref, mlq_ref, mlkt_ref, mlv_ref, mlo_ref, gates_ref,
                 *, tiles_per_seq):
    tm = x_ref.shape[0]
    w = ML_WIDTH
    u = _rms(x_ref[...], g_ref[...]).astype(BF16)
    qk = _dot(u, wml_ref[:, 0:2 * w])
    u_prev = _rms(xprev_ref[...], g_ref[...]).astype(BF16)
    qk_prev = _dot(u_prev, wml_ref[:, 0:2 * w])

    sbq_ref[...] = (_dot(u, wq_ref[...]) * (LOG2E * SB_HEAD_DIM ** -0.5)).astype(BF16)
    sbkt_ref[...] = _dot_nt(wkt_ref[...], u).astype(BF16)
    sbv_ref[...] = _dot(u, wv_ref[...]).astype(BF16)
    mlv_ref[...] = _dot(u, wml_ref[:, 2 * w:3 * w]).astype(BF16)
    mlo_ref[...] = _dot(u, wml_ref[:, 3 * w:4 * w])
    gates_ref[...] = _dot_nt(wgate_ref[...], u)

    first = pl.program_id(0) % tiles_per_seq == 0
    ext = jnp.concatenate([jnp.where(first, 0.0, qk_prev), qk], axis=0)
    y = bconv_ref[...]
    for j in range(CONV_WIDTH):
        s0 = SUBLANES - (CONV_WIDTH - 1) + j
        y = y + ext[s0:s0 + tm] * wconv_ref[j:j + 1, :]
    y = y * jax.nn.sigmoid(y)
    mlq_ref[...] = (y[:, 0:w] * (ML_HEAD_DIM ** -0.5)).astype(BF16)
    mlkt_ref[...] = y[:, w:2 * w].T.astype(BF16)


def _proj(x, g, wq, wkt, wv, wml, wgate, wconv, bconv, seq):
    t, d = x.shape
    tm = min(PROJ_TM, seq)
    tiles_per_seq = seq // tm
    row = lambda i: (i, 0)
    fixed = lambda i: (0, 0)
    whole = lambda a: pl.BlockSpec(a.shape, fixed, pipeline_mode=pl.Buffered(1))
    return pl.pallas_call(
        functools.partial(_proj_kernel, tiles_per_seq=tiles_per_seq),
        grid=(t // tm,),
        in_specs=[
            pl.BlockSpec((tm, d), row),
            pl.BlockSpec((SUBLANES, d), lambda i: (jnp.maximum(i * (tm // SUBLANES) - 1, 0), 0)),
            pl.BlockSpec((1, d), fixed),
            whole(wq), whole(wkt), whole(wv), whole(wml), whole(wgate), whole(wconv), whole(bconv),
        ],
        out_specs=[
            pl.BlockSpec((tm, SB_WIDTH), row),
            pl.BlockSpec((SB_WIDTH, tm), lambda i: (0, i)),
            pl.BlockSpec((tm, SB_WIDTH), row),
            pl.BlockSpec((tm, ML_WIDTH), row),
            pl.BlockSpec((None, ML_WIDTH, tm), lambda i: (i // tiles_per_seq, 0, i % tiles_per_seq)),
            pl.BlockSpec((tm, ML_WIDTH), row),
            pl.BlockSpec((tm, ML_WIDTH), row),
            pl.BlockSpec((None, 2 * ML_HEADS, tm), lambda i: (i // tiles_per_seq, 0, i % tiles_per_seq)),
        ],
        out_shape=[
            jax.ShapeDtypeStruct((t, SB_WIDTH), BF16),
            jax.ShapeDtypeStruct((SB_WIDTH, t), BF16),
            jax.ShapeDtypeStruct((t, SB_WIDTH), BF16),
            jax.ShapeDtypeStruct((t, ML_WIDTH), BF16),
            jax.ShapeDtypeStruct((t // seq, ML_WIDTH, seq), BF16),
            jax.ShapeDtypeStruct((t, ML_WIDTH), BF16),
            jax.ShapeDtypeStruct((t, ML_WIDTH), F32),
            jax.ShapeDtypeStruct((t // seq, 2 * ML_HEADS, seq), F32),
        ],
        compiler_params=_params(("parallel",)),
        name="proj",
    )(x, x, g.reshape(1, d), wq, wkt, wv, wml, wgate, wconv, bconv)


def _sb_kernel(q_ref, kt_ref, v_ref, o_ref, acc_ref, *, tq, tk, pairs):
    i = pl.program_id(2)
    half = SB_HEAD_DIM
    n_heads = 2 * pairs
    lane = lax.broadcasted_iota(jnp.int32, (tq, LANES), 1)
    q_heads = []
    for p in range(pairs):
        q = q_ref[:, p * LANES:(p + 1) * LANES]
        zero = jnp.zeros_like(q)
        q_heads += [jnp.where(lane < half, q, zero), jnp.where(lane >= half, q, zero)]

    r = lax.broadcasted_iota(jnp.int32, (2 * tk, tk), 0)
    c = lax.broadcasted_iota(jnp.int32, (2 * tk, tk), 1)
    suffix_ones = jnp.where((r & (tk - 1)) > c, 1.0, 0.0).astype(BF16)
    tr = lax.broadcasted_iota(jnp.int32, (tq, tk), 0)
    tc = lax.broadcasted_iota(jnp.int32, (tq, tk), 1)
    strict = tc < tr

    acc_ref[...] = jnp.zeros_like(acc_ref)

    def block(kb, carries, masked):
        start = pl.multiple_of(kb * tk, tk)
        pair = lambda h: slice((h // 2) * LANES, (h // 2 + 1) * LANES)
        z2, nl, later = {}, {}, {}
        new_carries = [None] * n_heads

        def scores(h):
            z2[h] = _dot(q_heads[h], kt_ref[pair(h), pl.ds(start, tk)])

        def suffix_sums(h):
            z = z2[h]
            x = jnp.maximum(z, 0.0) + jnp.log(1.0 + jnp.exp2(-jnp.abs(z))) * LOG2E
            if masked:
                x = jnp.where(strict, x, 0.0)
            hi = x.astype(BF16)
            lo = (x - hi.astype(F32)).astype(BF16)
            nl[h] = x
            later[h] = _dot(jnp.concatenate([hi, lo], axis=1), suffix_ones)

        def weights_times_values(h):
            a = jnp.exp2(z2.pop(h) - (nl[h] + later[h] + carries[h]))
            if masked:
                a = jnp.where(strict, a, 0.0)
            acc_ref[h] += _dot(a.astype(BF16), v_ref[pl.ds(start, tk), pair(h)])
            new_carries[h] = carries[h] + later.pop(h)[:, 0:1] + nl.pop(h)[:, 0:1]

        for step in range(n_heads + 2 * SB_STAGE_SKEW):
            if step < n_heads:
                scores(step)
            if 0 <= step - SB_STAGE_SKEW < n_heads:
                suffix_sums(step - SB_STAGE_SKEW)
            if 0 <= step - 2 * SB_STAGE_SKEW < n_heads:
                weights_times_values(step - 2 * SB_STAGE_SKEW)
        return tuple(new_carries)

    zeros = jnp.zeros((tq, 1), F32)
    carries = block(i, (zeros,) * n_heads, True)

    def body(n, carries):
        return block(i - 1 - n, carries, False)

    lax.fori_loop(0, i, body, carries)
    for p in range(pairs):
        o_ref[:, p * LANES:(p + 1) * LANES] = jnp.where(
            lane < half, acc_ref[2 * p], acc_ref[2 * p + 1]).astype(o_ref.dtype)


def _sb_attention(sbq, sbkt, sbv, batch, seq):
    t = sbq.shape[0]
    tq, tk = min(SB_TQ, seq), min(SB_TK, seq)
    assert tq == tk
    nq = seq // tq
    pairs = SB_PAIRS_PER_STEP
    width = pairs * LANES
    groups = SB_WIDTH // width
    return pl.pallas_call(
        functools.partial(_sb_kernel, tq=tq, tk=tk, pairs=pairs),
        grid=(batch, groups, nq),
        in_specs=[
            pl.BlockSpec((tq, width), lambda b, p, i: (b * nq + i, p)),
            pl.BlockSpec((width, seq), lambda b, p, i: (p, b)),
            pl.BlockSpec((seq, width), lambda b, p, i: (b, p)),
        ],
        out_specs=pl.BlockSpec((tq, width), lambda b, p, i: (b * nq + i, p)),
        out_shape=jax.ShapeDtypeStruct((t, SB_WIDTH), BF16),
        scratch_shapes=[pltpu.VMEM((2 * pairs, tq, LANES), F32)],
        compiler_params=_params(("parallel", "parallel", "arbitrary")),
        name="sb_attention",
    )(sbq, sbkt, sbv)


def _mlstm_kernel(q_ref, kt_ref, v_ref, og_ref, grow_ref, brow_ref,
                  ghead_ref, o_ref, c_ref, m_ref, *, n_batch, n_chunks):
    L, d, H = ML_KERNEL_CHUNK, ML_HEAD_DIM, ML_HEADS
    chains = [(b, h) for b in range(n_batch) for h in range(H)]

    @pl.when(pl.program_id(1) == 0)
    def _():
        c_ref[...] = jnp.zeros_like(c_ref)
        m_ref[...] = jnp.zeros_like(m_ref)

    rr = lax.broadcasted_iota(jnp.int32, (L, L), 0)
    cc = lax.broadcasted_iota(jnp.int32, (L, L), 1)
    causal = cc <= rr
    ones = jnp.ones((L, d), BF16)
    gate_lane = lax.broadcasted_iota(jnp.int32, (2 * H, L), 1)

    def chunk(ci, ms):
        rows = pl.ds(pl.multiple_of(ci * L, L), L)
        lf_row, lf_scan, g_row = {}, {}, {}
        for b in range(n_batch):
            g_row[b] = grow_ref[b, :, rows] + brow_ref[...]
            lf_row[b] = _log_sigmoid(g_row[b])
            scan = lf_row[b]
            for k in range(L.bit_length() - 1):
                shifted = pltpu.roll(scan, 1 << k, axis=1)
                scan = scan + jnp.where(gate_lane >= (1 << k), shifted, 0.0)
            lf_scan[b] = scan

        qb, kt, v1, c_st, s_raw, q_c = {}, {}, {}, {}, {}, {}
        for ch in chains:
            b, h = ch
            cols = slice(h * d, (h + 1) * d)
            qb[ch] = q_ref[b, rows, cols]
            kt[ch] = kt_ref[b, cols, rows]
            v1[ch] = jnp.concatenate([v_ref[b, rows, cols], ones], axis=1)
            c_st[ch] = c_ref[b * H + h]
            s_raw[ch] = _dot(qb[ch], kt[ch])
            q_c[ch] = _dot(qb[ch], c_st[ch].astype(BF16))

        bcum_c, bcum_r, li_row = {}, {}, {}
        for ch in chains:
            b, h = ch
            li_row[ch] = g_row[b][h:h + 1, :]
            lf_r = lf_row[b][H + h:H + h + 1, :]
            bcum_c[ch] = jnp.sum(jnp.where(causal, lf_r, 0.0), axis=1, keepdims=True)
            bcum_r[ch] = lf_scan[b][H + h:H + h + 1, :]

        dmat, dmax = {}, {}
        for ch in chains:
            dmat[ch] = jnp.where(causal, bcum_c[ch] - bcum_r[ch] + li_row[ch], -jnp.inf)
            dmax[ch] = jnp.max(dmat[ch], axis=1, keepdims=True)

        w_inter, m_t, intra, w_state, decay, new_ms = {}, {}, {}, {}, {}, []
        for idx, ch in enumerate(chains):
            btot = bcum_c[ch][L - 1:L, :]
            m_prev = ms[idx]
            inter = bcum_c.pop(ch) + m_prev
            m_t[ch] = jnp.maximum(dmax.pop(ch), inter)
            w_intra = jnp.exp(dmat.pop(ch) - m_t[ch])
            w_inter[ch] = jnp.exp(inter - m_t[ch])
            sc = s_raw.pop(ch) * w_intra
            intra[ch] = _dot(sc.astype(BF16), v1[ch])
            m_new = m_t[ch][L - 1:L, :]
            w_state[ch] = jnp.exp(btot - bcum_r.pop(ch) + li_row.pop(ch) - m_new)
            decay[ch] = jnp.exp(btot + m_prev - m_new)
            new_ms.append(m_new)

        for ch in chains:
            b, h = ch
            ktw = (kt.pop(ch).astype(F32) * w_state.pop(ch)).astype(BF16)
            c_ref[b * H + h] = decay.pop(ch) * c_st.pop(ch) + _dot(ktw, v1.pop(ch))

        hval, ssq = {}, {}
        for ch in chains:
            both = intra.pop(ch) + w_inter.pop(ch) * q_c.pop(ch)
            den = both[:, d:2 * d]
            hval[ch] = both[:, 0:d] / jnp.maximum(jnp.abs(den), jnp.exp(-m_t.pop(ch)))
        for ch in chains:
            ssq[ch] = jnp.mean(hval[ch] * hval[ch], axis=-1, keepdims=True)
        for ch in chains:
            b, h = ch
            cols = slice(h * d, (h + 1) * d)
            hn = hval.pop(ch) * lax.rsqrt(ssq.pop(ch) + EPS) * ghead_ref[:, cols]
            og = og_ref[b, rows, cols]
            o_ref[b, rows, cols] = (hn * jax.nn.sigmoid(og)).astype(o_ref.dtype)
        return tuple(new_ms)

    ms = tuple(m_ref[i] for i in range(len(chains)))
    ms = lax.fori_loop(0, n_chunks, chunk, ms)
    for i, m in enumerate(ms):
        m_ref[i] = m


def _mlstm(mlq, mlkt, mlv, mlo, grow, brow, ghead):
    batch, seq, width = mlq.shape
    nb = min(ML_BATCH_PER_STEP, batch)
    sblk = min(ML_SEQ_BLOCK, seq)
    chunks = sblk // ML_KERNEL_CHUNK
    n_gate = 2 * ML_HEADS
    seq_blk = pl.BlockSpec((nb, sblk, width), lambda b, s: (b, s, 0))
    fixed2 = lambda b, s: (0, 0)
    return pl.pallas_call(
        functools.partial(_mlstm_kernel, n_batch=nb, n_chunks=chunks),
        grid=(batch // nb, seq // sblk),
        in_specs=[
            seq_blk,
            pl.BlockSpec((nb, width, sblk), lambda b, s: (b, 0, s)),
            seq_blk, seq_blk,
            pl.BlockSpec((nb, n_gate, sblk), lambda b, s: (b, 0, s)),
            pl.BlockSpec(brow.shape, fixed2),
            pl.BlockSpec(ghead.shape, fixed2),
        ],
        out_specs=seq_blk,
        out_shape=jax.ShapeDtypeStruct((batch, seq, width), BF16),
        scratch_shapes=[pltpu.VMEM((nb * ML_HEADS, ML_HEAD_DIM, 2 * ML_HEAD_DIM), F32),
                        pltpu.VMEM((nb * ML_HEADS, 1, 1), F32)],
        compiler_params=_params(("parallel", "arbitrary")),
        name="mlstm",
    )(mlq, mlkt, mlv, mlo, grow, brow, ghead)


def _memkv_kernel(mem_ref, gm_ref, wk_ref, wv_ref, gk_ref, k_ref, v_ref):
    mn = _rms(mem_ref[...], gm_ref[...]).astype(BF16)
    k = _dot(mn, wk_ref[...])
    hd = k.shape[1] // X_HEADS
    for h in range(X_HEADS):
        k_ref[:, h * hd:(h + 1) * hd] = _rms(k[:, h * hd:(h + 1) * hd], gk_ref[...]).astype(BF16)
    v_ref[...] = _dot(mn, wv_ref[...]).astype(BF16)


def _memkv(mem2d, gm, wk, wv, gk, batch, layer):
    rows, d = mem2d.shape
    m = rows // batch
    blk = pl.BlockSpec((m, d), lambda b: (b, 0))
    fixed = lambda b: (0, 0)
    return pl.pallas_call(
        _memkv_kernel,
        grid=(batch,),
        in_specs=[blk, pl.BlockSpec((1, d), fixed), _layer_weight((d, d), layer),
                  _layer_weight((d, d), layer), pl.BlockSpec((1, d // X_HEADS), fixed)],
        out_specs=[blk, blk],
        out_shape=[jax.ShapeDtypeStruct((rows, d), BF16)] * 2,
        compiler_params=_params(("parallel",)),
        name="memkv",
    )(mem2d, gm.reshape(1, d), wk, wv, gk.reshape(1, d // X_HEADS))


def _post_kernel(x_ref, sb_ref, ml_ref, wo_ref, gx_ref, wq_ref, gq_ref, k_ref, v_ref, wxo_ref,
                 o_ref):
    d = x_ref.shape[1]
    hd = d // X_HEADS
    x1 = (x_ref[...] + _dot(sb_ref[...], wo_ref[0:SB_WIDTH, :])
          + _dot(ml_ref[...], wo_ref[SB_WIDTH:SB_WIDTH + ML_WIDTH, :]))
    u = _rms(x1, gx_ref[...]).astype(BF16)
    q = _dot(u, wq_ref[...])
    outs = []
    for h in range(X_HEADS):
        cols = slice(h * hd, (h + 1) * hd)
        qn = _rms(q[:, cols], gq_ref[...]).astype(BF16)
        s = _dot_nt(qn, k_ref[:, cols]) * (hd ** -0.5)
        e = jnp.exp(s - jnp.max(s, axis=-1, keepdims=True))
        o = _dot(e.astype(BF16), v_ref[:, cols]) / jnp.sum(e, axis=-1, keepdims=True)
        outs.append(o.astype(BF16))
    o_ref[...] = x1 + _dot(jnp.concatenate(outs, axis=1), wxo_ref[...])


def _post(x, sb, ml, wo, gx, wxq, gq, kmem, vmem, wxo, seq, layer):
    t, d = x.shape
    tm = min(POST_TM, seq)
    per_seq = seq // tm
    m = kmem.shape[0] // (t // seq)
    row = lambda i: (i, 0)
    fixed = lambda i: (0, 0)
    mem_blk = pl.BlockSpec((m, d), lambda i: (i // per_seq, 0))
    return pl.pallas_call(
        _post_kernel,
        grid=(t // tm,),
        in_specs=[
            pl.BlockSpec((tm, d), row),
            pl.BlockSpec((tm, SB_WIDTH), row),
            pl.BlockSpec((tm, ML_WIDTH), row),
            _layer_weight((d, d), layer),
            pl.BlockSpec((1, d), fixed),
            _layer_weight((d, d), layer),
            pl.BlockSpec((1, d // X_HEADS), fixed),
            mem_blk, mem_blk,
            _layer_weight((d, d), layer),
        ],
        out_specs=pl.BlockSpec((tm, d), row),
        out_shape=jax.ShapeDtypeStruct((t, d), F32),
        compiler_params=_params(("parallel",)),
        name="post",
    )(x, sb, ml, wo, gx.reshape(1, d), wxq, gq.reshape(1, d // X_HEADS), kmem, vmem, wxo)


def _layer(x, mem2d, batch, seq, layer, p):
    l = layer
    x = _ffn(x, p["g_ff1"][l], p["w_ff1_gate"], p["w_ff1_up"], p["w_ff1_down"], l)

    s, w = SB_WIDTH, ML_WIDTH
    n_gate = 2 * ML_HEADS
    w_in = p["w_in"][l]
    sbq, sbkt, sbv, mlq, mlkt, mlv, mlo, gates = _proj(
        x, p["g_mix"][l], w_in[:, 0:s], w_in[:, s:2 * s].T, w_in[:, 2 * s:3 * s],
        w_in[:, 3 * s:3 * s + 4 * w], w_in[:, 3 * s + 4 * w:].T, p["w_conv"][l],
        p["b_conv"][l].reshape(1, 2 * w), seq)

    sb = _sb_attention(sbq, sbkt, sbv, batch, seq)

    seq3 = lambda a: a.reshape(batch, seq, w)
    ml = _mlstm(seq3(mlq), mlkt, seq3(mlv), seq3(mlo), gates, p["b_gate"][l].reshape(n_gate, 1),
                p["g_mlstm_head"][l].reshape(1, w)).reshape(batch * seq, w)

    kmem, vmem = _memkv(mem2d, p["g_mem"][l], p["w_xk"], p["w_xv"], p["g_knorm"][l], batch, l)
    x = _post(x, sb, ml, p["w_out"], p["g_xattn"][l], p["w_xq"], p["g_qnorm"][l], kmem, vmem,
              p["w_xo"], seq, l)
    x = _ffn(x, p["g_ff2"][l], p["w_ff2_gate"], p["w_ff2_up"], p["w_ff2_down"], l)
    return x


_MATMUL_WEIGHTS = ("w_ff1_gate", "w_ff1_up", "w_ff1_down", "w_in", "w_out", "w_xq", "w_xk", "w_xv",
                   "w_xo", "w_ff2_gate", "w_ff2_up", "w_ff2_down")


def kernel(x, mem, g_ff1, w_ff1_gate, w_ff1_up, w_ff1_down, g_mix, w_in, b_gate, w_conv, b_conv, g_mlstm_head, w_out, g_xattn, g_mem, w_xq, w_xk, w_xv, g_qnorm, g_knorm, w_xo, g_ff2, w_ff2_gate, w_ff2_up, w_ff2_down):
    batch, seq, d = x.shape
    depth = g_ff1.shape[0]
    p = dict(g_ff1=g_ff1, w_ff1_gate=w_ff1_gate, w_ff1_up=w_ff1_up, w_ff1_down=w_ff1_down,
             g_mix=g_mix, w_in=w_in, b_gate=b_gate, w_conv=w_conv, b_conv=b_conv,
             g_mlstm_head=g_mlstm_head, w_out=w_out, g_xattn=g_xattn, g_mem=g_mem, w_xq=w_xq,
             w_xk=w_xk, w_xv=w_xv, g_qnorm=g_qnorm, g_knorm=g_knorm, w_xo=w_xo, g_ff2=g_ff2,
             w_ff2_gate=w_ff2_gate, w_ff2_up=w_ff2_up, w_ff2_down=w_ff2_down)
    for name in _MATMUL_WEIGHTS:
        p[name] = p[name].astype(BF16)
    h = x.reshape(batch * seq, d)
    mem2d = mem.reshape(batch * mem.shape[1], d)
    for l in range(depth):
        h = _layer(h, mem2d, batch, seq, l, p)
    return h.reshape(batch, seq, d)
```

```python
import functools

import jax
import jax.numpy as jnp
from jax import lax
from jax.experimental import pallas as pl
from jax.experimental.pallas import tpu as pltpu

F32 = jnp.float32
BF16 = jnp.bfloat16

EPS = 1e-6
LOG2E = 1.4426950408889634
SB_HEADS = 8
SB_HEAD_DIM = 64
SB_WIDTH = SB_HEADS * SB_HEAD_DIM
ML_HEADS = 4
ML_HEAD_DIM = 128
ML_WIDTH = ML_HEADS * ML_HEAD_DIM
ML_KERNEL_CHUNK = 128
CONV_WIDTH = 4
X_HEADS = 4

V7X_VMEM_BYTES = 64 * 1024 * 1024
VMEM_LIMIT_BYTES = V7X_VMEM_BYTES - 8 * 1024 * 1024
LANES = 128
SUBLANES = 8

FFN_TM = 512
FFN_TF = 256
PROJ_TM = 512
POST_TM = 512
SB_TQ = 256
SB_TK = 256
SB_PAIRS_PER_STEP = 4
SB_STAGE_SKEW = 1
SB_STICK_GONE_LOG2 = 152.0
ML_BATCH_PER_STEP = 2
ML_SEQ_BLOCK = 512


def _params(semantics):
    return pltpu.CompilerParams(dimension_semantics=semantics,
                                vmem_limit_bytes=VMEM_LIMIT_BYTES)


def _rms(x, g):
    return x * lax.rsqrt(jnp.mean(x * x, axis=-1, keepdims=True) + EPS) * g


def _dot(a, b):
    return jnp.dot(a, b, preferred_element_type=F32)


def _dot_nt(a, b):
    return lax.dot_general(a, b, (((1,), (1,)), ((), ())), preferred_element_type=F32)


def _log_sigmoid(x):
    return -(jnp.maximum(-x, 0.0) + jnp.log1p(jnp.exp(-jnp.abs(x))))


def _layer_weight(shape, layer):
    return pl.BlockSpec((None,) + shape, lambda *_: (layer, 0, 0), pipeline_mode=pl.Buffered(1))


def _ffn_kernel(x_ref, g_ref, wg_ref, wu_ref, wd_ref, o_ref, *, tf):
    x = x_ref[...]
    xn = _rms(x, g_ref[...]).astype(BF16)
    hidden = []
    for c in range(wg_ref.shape[1] // tf):
        cols = slice(c * tf, (c + 1) * tf)
        gate = _dot(xn, wg_ref[:, cols])
        up = _dot(xn, wu_ref[:, cols])
        hidden.append((gate * jax.nn.sigmoid(gate) * up).astype(BF16))
    o_ref[...] = x + 0.5 * _dot(jnp.concatenate(hidden, axis=1), wd_ref[...])


def _ffn(x, g, wg, wu, wd, layer):
    t, d = x.shape
    f = wg.shape[2]
    tm, tf = min(FFN_TM, t), min(FFN_TF, f)
    return pl.pallas_call(
        functools.partial(_ffn_kernel, tf=tf),
        grid=(t // tm,),
        in_specs=[
            pl.BlockSpec((tm, d), lambda i: (i, 0)),
            pl.BlockSpec((1, d), lambda i: (0, 0)),
            _layer_weight((d, f), layer), _layer_weight((d, f), layer),
            _layer_weight((f, d), layer),
        ],
        out_specs=pl.BlockSpec((tm, d), lambda i: (i, 0)),
        out_shape=jax.ShapeDtypeStruct((t, d), F32),
        compiler_params=_params(("parallel",)),
        name="ffn",
    )(x, g.reshape(1, d), wg, wu, wd)


def _proj_kernel(x_ref, xprev_ref, g_ref, wq_ref, wkt_ref, wv_ref, wml_ref, wgate_ref,
                 wconv_ref, bconv_ref,
                 sbq_ref, sbkt_ref, sbv_ref, mlq_ref, mlkt_ref, mlv_ref, mlo_ref, gates_ref,
                 ext_ref, *, tiles_per_seq):
    tm = x_ref.shape[0]
    w = ML_WIDTH
    u = _rms(x_ref[...], g_ref[...]).astype(BF16)
    ext_ref[SUBLANES:, :] = _dot(u, wml_ref[:, 0:2 * w])

    sbq_ref[...] = (_dot(u, wq_ref[...]) * (LOG2E * SB_HEAD_DIM ** -0.5)).astype(BF16)
    sbkt_ref[...] = _dot_nt(wkt_ref[...], u).astype(BF16)

    u_prev = _rms(xprev_ref[...], g_ref[...]).astype(BF16)
    first = pl.program_id(0) % tiles_per_seq == 0
    ext_ref[0:SUBLANES, :] = jnp.where(first, 0.0, _dot(u_prev, wml_ref[:, 0:2 * w]))

    sbv_ref[...] = _dot(u, wv_ref[...]).astype(BF16)
    mlv_ref[...] = _dot(u, wml_ref[:, 2 * w:3 * w]).astype(BF16)
    mlo_ref[...] = _dot(u, wml_ref[:, 3 * w:4 * w])
    gates_ref[...] = _dot_nt(wgate_ref[...], u)

    y = bconv_ref[...]
    for j in range(CONV_WIDTH):
        s0 = SUBLANES - (CONV_WIDTH - 1) + j
        y = y + ext_ref[s0:s0 + tm, :] * wconv_ref[j:j + 1, :]
    y = y * jax.nn.sigmoid(y)
    mlq_ref[...] = (y[:, 0:w] * (ML_HEAD_DIM ** -0.5)).astype(BF16)
    mlkt_ref[...] = y[:, w:2 * w].T.astype(BF16)


def _proj(x, g, wq, wkt, wv, wml, wgate, wconv, bconv, seq):
    t, d = x.shape
    tm = min(PROJ_TM, seq)
    tiles_per_seq = seq // tm
    row = lambda i: (i, 0)
    fixed = lambda i: (0, 0)
    whole = lambda a: pl.BlockSpec(a.shape, fixed, pipeline_mode=pl.Buffered(1))
    return pl.pallas_call(
        functools.partial(_proj_kernel, tiles_per_seq=tiles_per_seq),
        grid=(t // tm,),
        in_specs=[
            pl.BlockSpec((tm, d), row),
            pl.BlockSpec((SUBLANES, d), lambda i: (jnp.maximum(i * (tm // SUBLANES) - 1, 0), 0)),
            pl.BlockSpec((1, d), fixed),
            whole(wq), whole(wkt), whole(wv), whole(wml), whole(wgate), whole(wconv), whole(bconv),
        ],
        out_specs=[
            pl.BlockSpec((tm, SB_WIDTH), row),
            pl.BlockSpec((SB_WIDTH, tm), lambda i: (0, i)),
            pl.BlockSpec((tm, SB_WIDTH), row),
            pl.BlockSpec((tm, ML_WIDTH), row),
            pl.BlockSpec((None, ML_WIDTH, tm), lambda i: (i // tiles_per_seq, 0, i % tiles_per_seq)),
            pl.BlockSpec((tm, ML_WIDTH), row),
            pl.BlockSpec((tm, ML_WIDTH), row),
            pl.BlockSpec((None, 2 * ML_HEADS, tm), lambda i: (i // tiles_per_seq, 0, i % tiles_per_seq)),
        ],
        out_shape=[
            jax.ShapeDtypeStruct((t, SB_WIDTH), BF16),
            jax.ShapeDtypeStruct((SB_WIDTH, t), BF16),
            jax.ShapeDtypeStruct((t, SB_WIDTH), BF16),
            jax.ShapeDtypeStruct((t, ML_WIDTH), BF16),
            jax.ShapeDtypeStruct((t // seq, ML_WIDTH, seq), BF16),
            jax.ShapeDtypeStruct((t, ML_WIDTH), BF16),
            jax.ShapeDtypeStruct((t, ML_WIDTH), F32),
            jax.ShapeDtypeStruct((t // seq, 2 * ML_HEADS, seq), F32),
        ],
        scratch_shapes=[pltpu.VMEM((tm + SUBLANES, 2 * ML_WIDTH), F32)],
        compiler_params=_params(("parallel",)),
        name="proj",
    )(x, x, g.reshape(1, d), wq, wkt, wv, wml, wgate, wconv, bconv)


def _sb_kernel(q_ref, kt_ref, v_ref, o_ref, acc_ref, *, tq, tk, pairs):
    i = pl.program_id(2)
    half = SB_HEAD_DIM
    n_heads = 2 * pairs
    lane = lax.broadcasted_iota(jnp.int32, (tq, LANES), 1)
    q_heads = []
    for p in range(pairs):
        q = q_ref[:, p * LANES:(p + 1) * LANES]
        zero = jnp.zeros_like(q)
        q_heads += [jnp.where(lane < half, q, zero), jnp.where(lane >= half, q, zero)]

    r = lax.broadcasted_iota(jnp.int32, (2 * tk, tk), 0)
    c = lax.broadcasted_iota(jnp.int32, (2 * tk, tk), 1)
    suffix_ones = jnp.where((r & (tk - 1)) > c, 1.0, 0.0).astype(BF16)
    tr = lax.broadcasted_iota(jnp.int32, (tq, tk), 0)
    tc = lax.broadcasted_iota(jnp.int32, (tq, tk), 1)
    strict = tc < tr

    acc_ref[...] = jnp.zeros_like(acc_ref)

    def block(kb, carries, masked):
        start = pl.multiple_of(kb * tk, tk)
        pair = lambda h: slice((h // 2) * LANES, (h // 2 + 1) * LANES)
        z2, nl, later = {}, {}, {}
        new_carries = [None] * n_heads

        def scores(h):
            z2[h] = _dot(q_heads[h], kt_ref[pair(h), pl.ds(start, tk)])

        def suffix_sums(h):
            z = z2[h]
            x = jnp.maximum(z, 0.0) + jnp.log(1.0 + jnp.exp2(-jnp.abs(z))) * LOG2E
            if masked:
                x = jnp.where(strict, x, 0.0)
            hi = x.astype(BF16)
            lo = (x - hi.astype(F32)).astype(BF16)
            nl[h] = x
            later[h] = _dot(jnp.concatenate([hi, lo], axis=1), suffix_ones)

        def weights_times_values(h):
            a = jnp.exp2(z2.pop(h) - (nl[h] + later[h] + carries[h]))
            if masked:
                a = jnp.where(strict, a, 0.0)
            acc_ref[h] += _dot(a.astype(BF16), v_ref[pl.ds(start, tk), pair(h)])
            new_carries[h] = carries[h] + later.pop(h)[:, 0:1] + nl.pop(h)[:, 0:1]

        for step in range(n_heads + 2 * SB_STAGE_SKEW):
            if step < n_heads:
                scores(step)
            if 0 <= step - SB_STAGE_SKEW < n_heads:
                suffix_sums(step - SB_STAGE_SKEW)
            if 0 <= step - 2 * SB_STAGE_SKEW < n_heads:
                weights_times_values(step - 2 * SB_STAGE_SKEW)
        return tuple(new_carries)

    zeros = jnp.zeros((tq, 1), F32)
    carries = block(i, (zeros,) * n_heads, True)

    def live(state):
        n, carries = state
        least = functools.reduce(jnp.minimum, carries)
        return jnp.logical_and(n < i, jnp.min(least) < SB_STICK_GONE_LOG2)

    def body(state):
        n, carries = state
        return n + 1, block(i - 1 - n, carries, False)

    lax.while_loop(live, body, (jnp.int32(0), carries))
    for p in range(pairs):
        o_ref[:, p * LANES:(p + 1) * LANES] = jnp.where(
            lane < half, acc_ref[2 * p], acc_ref[2 * p + 1]).astype(o_ref.dtype)


def _sb_attention(sbq, sbkt, sbv, batch, seq):
    t = sbq.shape[0]
    tq, tk = min(SB_TQ, seq), min(SB_TK, seq)
    assert tq == tk
    nq = seq // tq
    pairs = SB_PAIRS_PER_STEP
    width = pairs * LANES
    groups = SB_WIDTH // width
    return pl.pallas_call(
        functools.partial(_sb_kernel, tq=tq, tk=tk, pairs=pairs),
        grid=(batch, groups, nq),
        in_specs=[
            pl.BlockSpec((tq, width), lambda b, p, i: (b * nq + i, p)),
            pl.BlockSpec((width, seq), lambda b, p, i: (p, b)),
            pl.BlockSpec((seq, width), lambda b, p, i: (b, p)),
        ],
        out_specs=pl.BlockSpec((tq, width), lambda b, p, i: (b * nq + i, p)),
        out_shape=jax.ShapeDtypeStruct((t, SB_WIDTH), BF16),
        scratch_shapes=[pltpu.VMEM((2 * pairs, tq, LANES), F32)],
        compiler_params=_params(("parallel", "parallel", "arbitrary")),
        name="sb_attention",
    )(sbq, sbkt, sbv)


def _mlstm_kernel(q_ref, kt_ref, v_ref, og_ref, grow_ref, brow_ref,
                  ghead_ref, o_ref, c_ref, m_ref, *, n_batch, n_chunks):
    L, d, H = ML_KERNEL_CHUNK, ML_HEAD_DIM, ML_HEADS
    chains = [(b, h) for b in range(n_batch) for h in range(H)]

    @pl.when(pl.program_id(1) == 0)
    def _():
        c_ref[...] = jnp.zeros_like(c_ref)
        m_ref[...] = jnp.zeros_like(m_ref)

    rr = lax.broadcasted_iota(jnp.int32, (L, L), 0)
    cc = lax.broadcasted_iota(jnp.int32, (L, L), 1)
    causal = cc <= rr
    ones = jnp.ones((L, d), BF16)
    gate_lane = lax.broadcasted_iota(jnp.int32, (2 * H, L), 1)

    def chunk(ci, ms):
        rows = pl.ds(pl.multiple_of(ci * L, L), L)
        lf_row, lf_scan, g_row = {}, {}, {}
        for b in range(n_batch):
            g_row[b] = grow_ref[b, :, rows] + brow_ref[...]
            lf_row[b] = _log_sigmoid(g_row[b])
            scan = lf_row[b]
            for k in range(L.bit_length() - 1):
                shifted = pltpu.roll(scan, 1 << k, axis=1)
                scan = scan + jnp.where(gate_lane >= (1 << k), shifted, 0.0)
            lf_scan[b] = scan

        qb, kt, v1, c_st, s_raw, q_c = {}, {}, {}, {}, {}, {}
        for ch in chains:
            b, h = ch
            cols = slice(h * d, (h + 1) * d)
            qb[ch] = q_ref[b, rows, cols]
            kt[ch] = kt_ref[b, cols, rows]
            v1[ch] = jnp.concatenate([v_ref[b, rows, cols], ones], axis=1)
            c_st[ch] = c_ref[b * H + h]
            s_raw[ch] = _dot(qb[ch], kt[ch])
            q_c[ch] = _dot(qb[ch], c_st[ch].astype(BF16))

        bcum_c, bcum_r, li_row = {}, {}, {}
        for ch in chains:
            b, h = ch
            li_row[ch] = g_row[b][h:h + 1, :]
            lf_r = lf_row[b][H + h:H + h + 1, :]
            bcum_c[ch] = jnp.sum(jnp.where(causal, lf_r, 0.0), axis=1, keepdims=True)
            bcum_r[ch] = lf_scan[b][H + h:H + h + 1, :]

        dmat, dmax = {}, {}
        for ch in chains:
            dmat[ch] = jnp.where(causal, bcum_c[ch] - bcum_r[ch] + li_row[ch], -jnp.inf)
            dmax[ch] = jnp.max(dmat[ch], axis=1, keepdims=True)

        w_inter, m_t, intra, w_state, decay, new_ms = {}, {}, {}, {}, {}, []
        for idx, ch in enumerate(chains):
            btot = bcum_c[ch][L - 1:L, :]
            m_prev = ms[idx]
            inter = bcum_c.pop(ch) + m_prev
            m_t[ch] = jnp.maximum(dmax.pop(ch), inter)
            w_intra = jnp.exp(dmat.pop(ch) - m_t[ch])
            w_inter[ch] = jnp.exp(inter - m_t[ch])
            sc = s_raw.pop(ch) * w_intra
            intra[ch] = _dot(sc.astype(BF16), v1[ch])
            m_new = m_t[ch][L - 1:L, :]
            w_state[ch] = jnp.exp(btot - bcum_r.pop(ch) + li_row.pop(ch) - m_new)
            decay[ch] = jnp.exp(btot + m_prev - m_new)
            new_ms.append(m_new)

        for ch in chains:
            b, h = ch
            ktw = (kt.pop(ch).astype(F32) * w_state.pop(ch)).astype(BF16)
            c_ref[b * H + h] = decay.pop(ch) * c_st.pop(ch) + _dot(ktw, v1.pop(ch))

        hval, ssq = {}, {}
        for ch in chains:
            both = intra.pop(ch) + w_inter.pop(ch) * q_c.pop(ch)
            den = both[:, d:2 * d]
            hval[ch] = both[:, 0:d] / jnp.maximum(jnp.abs(den), jnp.exp(-m_t.pop(ch)))
        for ch in chains:
            ssq[ch] = jnp.mean(hval[ch] * hval[ch], axis=-1, keepdims=True)
        for ch in chains:
            b, h = ch
            cols = slice(h * d, (h + 1) * d)
            hn = hval.pop(ch) * lax.rsqrt(ssq.pop(ch) + EPS) * ghead_ref[:, cols]
            og = og_ref[b, rows, cols]
            o_ref[b, rows, cols] = (hn * jax.nn.sigmoid(og)).astype(o_ref.dtype)
        return tuple(new_ms)

    ms = tuple(m_ref[i] for i in range(len(chains)))
    ms = lax.fori_loop(0, n_chunks, chunk, ms)
    for i, m in enumerate(ms):
        m_ref[i] = m


def _mlstm(mlq, mlkt, mlv, mlo, grow, brow, ghead):
    batch, seq, width = mlq.shape
    nb = min(ML_BATCH_PER_STEP, batch)
    sblk = min(ML_SEQ_BLOCK, seq)
    chunks = sblk // ML_KERNEL_CHUNK
    n_gate = 2 * ML_HEADS
    seq_blk = pl.BlockSpec((nb, sblk, width), lambda b, s: (b, s, 0))
    fixed2 = lambda b, s: (0, 0)
    return pl.pallas_call(
        functools.partial(_mlstm_kernel, n_batch=nb, n_chunks=chunks),
        grid=(batch // nb, seq // sblk),
        in_specs=[
            seq_blk,
            pl.BlockSpec((nb, width, sblk), lambda b, s: (b, 0, s)),
            seq_blk, seq_blk,
            pl.BlockSpec((nb, n_gate, sblk), lambda b, s: (b, 0, s)),
            pl.BlockSpec(brow.shape, fixed2),
            pl.BlockSpec(ghead.shape, fixed2),
        ],
        out_specs=seq_blk,
        out_shape=jax.ShapeDtypeStruct((batch, seq, width), BF16),
        scratch_shapes=[pltpu.VMEM((nb * ML_HEADS, ML_HEAD_DIM, 2 * ML_HEAD_DIM), F32),
                        pltpu.VMEM((nb * ML_HEADS, 1, 1), F32)],
        compiler_params=_params(("parallel", "arbitrary")),
        name="mlstm",
    )(mlq, mlkt, mlv, mlo, grow, brow, ghead)


def _memkv_kernel(mem_ref, gm_ref, wk_ref, wv_ref, gk_ref, k_ref, v_ref):
    mn = _rms(mem_ref[...], gm_ref[...]).astype(BF16)
    k = _dot(mn, wk_ref[...])
    hd = k.shape[1] // X_HEADS
    for h in range(X_HEADS):
        k_ref[:, h * hd:(h + 1) * hd] = _rms(k[:, h * hd:(h + 1) * hd], gk_ref[...]).astype(BF16)
    v_ref[...] = _dot(mn, wv_ref[...]).astype(BF16)


def _memkv(mem2d, gm, wk, wv, gk, batch, layer):
    rows, d = mem2d.shape
    m = rows // batch
    blk = pl.BlockSpec((m, d), lambda b: (b, 0))
    fixed = lambda b: (0, 0)
    return pl.pallas_call(
        _memkv_kernel,
        grid=(batch,),
        in_specs=[blk, pl.BlockSpec((1, d), fixed), _layer_weight((d, d), layer),
                  _layer_weight((d, d), layer), pl.BlockSpec((1, d // X_HEADS), fixed)],
        out_specs=[blk, blk],
        out_shape=[jax.ShapeDtypeStruct((rows, d), BF16)] * 2,
        compiler_params=_params(("parallel",)),
        name="memkv",
    )(mem2d, gm.reshape(1, d), wk, wv, gk.reshape(1, d // X_HEADS))


def _post_kernel(x_ref, sb_ref, ml_ref, wo_ref, gx_ref, wq_ref, gq_ref, k_ref, v_ref, wxo_ref,
                 o_ref):
    d = x_ref.shape[1]
    hd = d // X_HEADS
    x1 = (x_ref[...] + _dot(sb_ref[...], wo_ref[0:SB_WIDTH, :])
          + _dot(ml_ref[...], wo_ref[SB_WIDTH:SB_WIDTH + ML_WIDTH, :]))
    u = _rms(x1, gx_ref[...]).astype(BF16)
    q = _dot(u, wq_ref[...])
    outs = []
    for h in range(X_HEADS):
        cols = slice(h * hd, (h + 1) * hd)
        qn = _rms(q[:, cols], gq_ref[...]).astype(BF16)
        s = _dot_nt(qn, k_ref[:, cols]) * (hd ** -0.5)
        e = jnp.exp(s - jnp.max(s, axis=-1, keepdims=True))
        o = _dot(e.astype(BF16), v_ref[:, cols]) / jnp.sum(e, axis=-1, keepdims=True)
        outs.append(o.astype(BF16))
    o_ref[...] = x1 + _dot(jnp.concatenate(outs, axis=1), wxo_ref[...])


def _post(x, sb, ml, wo, gx, wxq, gq, kmem, vmem, wxo, seq, layer):
    t, d = x.shape
    tm = min(POST_TM, seq)
    per_seq = seq // tm
    m = kmem.shape[0] // (t // seq)
    row = lambda i: (i, 0)
    fixed = lambda i: (0, 0)
    mem_blk = pl.BlockSpec((m, d), lambda i: (i // per_seq, 0))
    return pl.pallas_call(
        _post_kernel,
        grid=(t // tm,),
        in_specs=[
            pl.BlockSpec((tm, d), row),
            pl.BlockSpec((tm, SB_WIDTH), row),
            pl.BlockSpec((tm, ML_WIDTH), row),
            _layer_weight((d, d), layer),
            pl.BlockSpec((1, d), fixed),
            _layer_weight((d, d), layer),
            pl.BlockSpec((1, d // X_HEADS), fixed),
            mem_blk, mem_blk,
            _layer_weight((d, d), layer),
        ],
        out_specs=pl.BlockSpec((tm, d), row),
        out_shape=jax.ShapeDtypeStruct((t, d), F32),
        compiler_params=_params(("parallel",)),
        name="post",
    )(x, sb, ml, wo, gx.reshape(1, d), wxq, gq.reshape(1, d // X_HEADS), kmem, vmem, wxo)


def _layer(x, mem2d, batch, seq, layer, p):
    l = layer
    x = _ffn(x, p["g_ff1"][l], p["w_ff1_gate"], p["w_ff1_up"], p["w_ff1_down"], l)

    s, w = SB_WIDTH, ML_WIDTH
    n_gate = 2 * ML_HEADS
    w_in = p["w_in"][l]
    sbq, sbkt, sbv, mlq, mlkt, mlv, mlo, gates = _proj(
        x, p["g_mix"][l], w_in[:, 0:s], w_in[:, s:2 * s].T, w_in[:, 2 * s:3 * s],
        w_in[:, 3 * s:3 * s + 4 * w], w_in[:, 3 * s + 4 * w:].T, p["w_conv"][l],
        p["b_conv"][l].reshape(1, 2 * w), seq)

    sb = _sb_attention(sbq, sbkt, sbv, batch, seq)

    seq3 = lambda a: a.reshape(batch, seq, w)
    ml = _mlstm(seq3(mlq), mlkt, seq3(mlv), seq3(mlo), gates, p["b_gate"][l].reshape(n_gate, 1),
                p["g_mlstm_head"][l].reshape(1, w)).reshape(batch * seq, w)

    kmem, vmem = _memkv(mem2d, p["g_mem"][l], p["w_xk"], p["w_xv"], p["g_knorm"][l], batch, l)
    x = _post(x, sb, ml, p["w_out"], p["g_xattn"][l], p["w_xq"], p["g_qnorm"][l], kmem, vmem,
              p["w_xo"], seq, l)
    x = _ffn(x, p["g_ff2"][l], p["w_ff2_gate"], p["w_ff2_up"], p["w_ff2_down"], l)
    return x


_MATMUL_WEIGHTS = ("w_ff1_gate", "w_ff1_up", "w_ff1_down", "w_in", "w_out", "w_xq", "w_xk", "w_xv",
                   "w_xo", "w_ff2_gate", "w_ff2_up", "w_ff2_down")


def kernel(x, mem, g_ff1, w_ff1_gate, w_ff1_up, w_ff1_down, g_mix, w_in, b_gate, w_conv, b_conv, g_mlstm_head, w_out, g_xattn, g_mem, w_xq, w_xk, w_xv, g_qnorm, g_knorm, w_xo, g_ff2, w_ff2_gate, w_ff2_up, w_ff2_down):
    batch, seq, d = x.shape
    depth = g_ff1.shape[0]
    p = dict(g_ff1=g_ff1, w_ff1_gate=w_ff1_gate, w_ff1_up=w_ff1_up, w_ff1_down=w_ff1_down,
             g_mix=g_mix, w_in=w_in, b_gate=b_gate, w_conv=w_conv, b_conv=b_conv,
             g_mlstm_head=g_mlstm_head, w_out=w_out, g_xattn=g_xattn, g_mem=g_mem, w_xq=w_xq,
             w_xk=w_xk, w_xv=w_xv, g_qnorm=g_qnorm, g_knorm=g_knorm, w_xo=w_xo, g_ff2=g_ff2,
             w_ff2_gate=w_ff2_gate, w_ff2_up=w_ff2_up, w_ff2_down=w_ff2_down)
    for name in _MATMUL_WEIGHTS:
        p[name] = p[name].astype(BF16)
    h = x.reshape(batch * seq, d)
    mem2d = mem.reshape(batch * mem.shape[1], d)
    for l in range(depth):
        h = _layer(h, mem2d, batch, seq, l, p)
    return h.reshape(batch, seq, d)
```

```python
import functools

import jax
import jax.numpy as jnp
from jax import lax
from jax.experimental import pallas as pl
from jax.experimental.pallas import tpu as pltpu

F32 = jnp.float32
BF16 = jnp.bfloat16

EPS = 1e-6
LOG2E = 1.4426950408889634
SB_HEADS = 8
SB_HEAD_DIM = 64
SB_WIDTH = SB_HEADS * SB_HEAD_DIM
ML_HEADS = 4
ML_HEAD_DIM = 128
ML_WIDTH = ML_HEADS * ML_HEAD_DIM
ML_KERNEL_CHUNK = 128
CONV_WIDTH = 4
X_HEADS = 4

V7X_VMEM_BYTES = 64 * 1024 * 1024
VMEM_LIMIT_BYTES = V7X_VMEM_BYTES - 8 * 1024 * 1024
LANES = 128
SUBLANES = 8

FFN_TM = 512
FFN_TF = 256
PROJ_TM = 512
POST_TM = 512
SB_TQ = 256
SB_TK = 256
SB_PAIRS_PER_STEP = 4
SB_STAGE_SKEW = 1
SB_STICK_GONE_LOG2 = 152.0
SB_TOP_ROWS = 176
ML_BATCH_PER_STEP = 4
ML_SEQ_BLOCK = 512


def _params(semantics):
    return pltpu.CompilerParams(dimension_semantics=semantics,
                                vmem_limit_bytes=VMEM_LIMIT_BYTES)


def _rms(x, g):
    return x * lax.rsqrt(jnp.mean(x * x, axis=-1, keepdims=True) + EPS) * g


def _dot(a, b):
    return jnp.dot(a, b, preferred_element_type=F32)


def _dot_nt(a, b):
    return lax.dot_general(a, b, (((1,), (1,)), ((), ())), preferred_element_type=F32)


def _log_sigmoid(x):
    return -(jnp.maximum(-x, 0.0) + jnp.log1p(jnp.exp(-jnp.abs(x))))


def _layer_weight(shape, layer):
    return pl.BlockSpec((None,) + shape, lambda *_: (layer, 0, 0), pipeline_mode=pl.Buffered(1))


def _ffn_kernel(x_ref, g_ref, wg_ref, wu_ref, wd_ref, o_ref, *, tf):
    x = x_ref[...]
    xn = _rms(x, g_ref[...]).astype(BF16)
    hidden = []
    for c in range(wg_ref.shape[1] // tf):
        cols = slice(c * tf, (c + 1) * tf)
        gate = _dot(xn, wg_ref[:, cols])
        up = _dot(xn, wu_ref[:, cols])
        hidden.append((gate * jax.nn.sigmoid(gate) * up).astype(BF16))
    o_ref[...] = x + 0.5 * _dot(jnp.concatenate(hidden, axis=1), wd_ref[...])


def _ffn(x, g, wg, wu, wd, layer):
    t, d = x.shape
    f = wg.shape[2]
    tm, tf = min(FFN_TM, t), min(FFN_TF, f)
    return pl.pallas_call(
        functools.partial(_ffn_kernel, tf=tf),
        grid=(t // tm,),
        in_specs=[
            pl.BlockSpec((tm, d), lambda i: (i, 0)),
            pl.BlockSpec((1, d), lambda i: (0, 0)),
            _layer_weight((d, f), layer), _layer_weight((d, f), layer),
            _layer_weight((f, d), layer),
        ],
        out_specs=pl.BlockSpec((tm, d), lambda i: (i, 0)),
        out_shape=jax.ShapeDtypeStruct((t, d), F32),
        compiler_params=_params(("parallel",)),
        name="ffn",
    )(x, g.reshape(1, d), wg, wu, wd)


def _proj_kernel(x_ref, xprev_ref, g_ref, wq_ref, wkt_ref, wv_ref, wml_ref, wgate_ref,
                 wconv_ref, bconv_ref,
                 sbq_ref, sbkt_ref, sbv_ref, mlq_ref, mlkt_ref, mlv_ref, mlo_ref, gates_ref,
                 ext_ref, *, tiles_per_seq):
    tm = x_ref.shape[0]
    w = ML_WIDTH
    u = _rms(x_ref[...], g_ref[...]).astype(BF16)
    ext_ref[SUBLANES:, :] = _dot(u, wml_ref[:, 0:2 * w])

    sbq_ref[...] = (_dot(u, wq_ref[...]) * (LOG2E * SB_HEAD_DIM ** -0.5)).astype(BF16)
    sbkt_ref[...] = _dot_nt(wkt_ref[...], u).astype(BF16)

    u_prev = _rms(xprev_ref[...], g_ref[...]).astype(BF16)
    first = pl.program_id(0) % tiles_per_seq == 0
    ext_ref[0:SUBLANES, :] = jnp.where(first, 0.0, _dot(u_prev, wml_ref[:, 0:2 * w]))

    sbv_ref[...] = _dot(u, wv_ref[...]).astype(BF16)
    mlv_ref[...] = _dot(u, wml_ref[:, 2 * w:3 * w]).astype(BF16)
    mlo_ref[...] = _dot(u, wml_ref[:, 3 * w:4 * w])
    gates_ref[...] = _dot_nt(wgate_ref[...], u)

    ext = ext_ref[...]
    y = bconv_ref[...]
    for j in range(CONV_WIDTH):
        back = CONV_WIDTH - 1 - j
        window = pltpu.roll(ext, back, axis=0) if back else ext
        y = y + window[SUBLANES:, :] * wconv_ref[j:j + 1, :]
    y = y * jax.nn.sigmoid(y)
    mlq_ref[...] = (y[:, 0:w] * (ML_HEAD_DIM ** -0.5)).astype(BF16)
    mlkt_ref[...] = y[:, w:2 * w].T.astype(BF16)


def _proj(x, g, wq, wkt, wv, wml, wgate, wconv, bconv, seq):
    t, d = x.shape
    tm = min(PROJ_TM, seq)
    tiles_per_seq = seq // tm
    row = lambda i: (i, 0)
    fixed = lambda i: (0, 0)
    whole = lambda a: pl.BlockSpec(a.shape, fixed, pipeline_mode=pl.Buffered(1))
    return pl.pallas_call(
        functools.partial(_proj_kernel, tiles_per_seq=tiles_per_seq),
        grid=(t // tm,),
        in_specs=[
            pl.BlockSpec((tm, d), row),
            pl.BlockSpec((SUBLANES, d), lambda i: (jnp.maximum(i * (tm // SUBLANES) - 1, 0), 0)),
            pl.BlockSpec((1, d), fixed),
            whole(wq), whole(wkt), whole(wv), whole(wml), whole(wgate), whole(wconv), whole(bconv),
        ],
        out_specs=[
            pl.BlockSpec((tm, SB_WIDTH), row),
            pl.BlockSpec((SB_WIDTH, tm), lambda i: (0, i)),
            pl.BlockSpec((tm, SB_WIDTH), row),
            pl.BlockSpec((tm, ML_WIDTH), row),
            pl.BlockSpec((None, ML_WIDTH, tm), lambda i: (i // tiles_per_seq, 0, i % tiles_per_seq)),
            pl.BlockSpec((tm, ML_WIDTH), row),
            pl.BlockSpec((tm, ML_WIDTH), row),
            pl.BlockSpec((None, 2 * ML_HEADS, tm), lambda i: (i // tiles_per_seq, 0, i % tiles_per_seq)),
        ],
        out_shape=[
            jax.ShapeDtypeStruct((t, SB_WIDTH), BF16),
            jax.ShapeDtypeStruct((SB_WIDTH, t), BF16),
            jax.ShapeDtypeStruct((t, SB_WIDTH), BF16),
            jax.ShapeDtypeStruct((t, ML_WIDTH), BF16),
            jax.ShapeDtypeStruct((t // seq, ML_WIDTH, seq), BF16),
            jax.ShapeDtypeStruct((t, ML_WIDTH), BF16),
            jax.ShapeDtypeStruct((t, ML_WIDTH), F32),
            jax.ShapeDtypeStruct((t // seq, 2 * ML_HEADS, seq), F32),
        ],
        scratch_shapes=[pltpu.VMEM((tm + SUBLANES, 2 * ML_WIDTH), F32)],
        compiler_params=_params(("parallel",)),
        name="proj",
    )(x, x, g.reshape(1, d), wq, wkt, wv, wml, wgate, wconv, bconv)


def _sb_kernel(q_ref, kt_ref, v_ref, o_ref, acc_ref, *, tq, tk, pairs):
    i = pl.program_id(2)
    half = SB_HEAD_DIM
    n_heads = 2 * pairs
    top = min(SB_TOP_ROWS, tq)
    lane = lax.broadcasted_iota(jnp.int32, (tq, LANES), 1)
    q_heads = []
    for p in range(pairs):
        q = q_ref[:, p * LANES:(p + 1) * LANES]
        zero = jnp.zeros_like(q)
        q_heads += [jnp.where(lane < half, q, zero), jnp.where(lane >= half, q, zero)]
    pair = lambda h: slice((h // 2) * LANES, (h // 2 + 1) * LANES)

    def suffix_ones(n):
        r = lax.broadcasted_iota(jnp.int32, (2 * n, n), 0)
        c = lax.broadcasted_iota(jnp.int32, (2 * n, n), 1)
        return jnp.where((r & (n - 1)) > c, 1.0, 0.0).astype(BF16)

    ones = {tk: suffix_ones(tk)}

    acc_ref[...] = jnp.zeros_like(acc_ref)

    def block(kb, carries, jobs, masked):
        start = pl.multiple_of(kb * tk, tk)
        chains = [(job, h) for job in jobs for h in range(n_heads)]
        z2, nl, later, fresh = {}, {}, {}, {}

        def strict(job):
            r0, r1, nk = job
            tr = lax.broadcasted_iota(jnp.int32, (r1 - r0, nk), 0) + r0
            tc = lax.broadcasted_iota(jnp.int32, (r1 - r0, nk), 1)
            return tc < tr

        def scores(ch):
            (r0, r1, nk), h = ch
            z2[ch] = _dot(q_heads[h][r0:r1], kt_ref[pair(h), pl.ds(start, nk)])

        def suffix_sums(ch):
            (r0, r1, nk), h = ch
            z = z2[ch]
            x = jnp.maximum(z, 0.0) + jnp.log(1.0 + jnp.exp2(-jnp.abs(z))) * LOG2E
            if masked:
                x = jnp.where(strict(ch[0]), x, 0.0)
            hi = x.astype(BF16)
            lo = (x - hi.astype(F32)).astype(BF16)
            nl[ch] = x
            later[ch] = _dot(jnp.concatenate([hi, lo], axis=1), ones[nk])

        def weights_times_values(ch):
            (r0, r1, nk), h = ch
            carry = carries[h][r0:r1]
            a = jnp.exp2(z2.pop(ch) - (nl[ch] + later[ch] + carry))
            if masked:
                a = jnp.where(strict(ch[0]), a, 0.0)
            acc_ref[h, r0:r1, :] += _dot(a.astype(BF16), v_ref[pl.ds(start, nk), pair(h)])
            fresh[ch] = carry + later.pop(ch)[:, 0:1] + nl.pop(ch)[:, 0:1]

        n = len(chains)
        for step in range(n + 2 * SB_STAGE_SKEW):
            if step < n:
                scores(chains[step])
            if 0 <= step - SB_STAGE_SKEW < n:
                suffix_sums(chains[step - SB_STAGE_SKEW])
            if 0 <= step - 2 * SB_STAGE_SKEW < n:
                weights_times_values(chains[step - 2 * SB_STAGE_SKEW])

        out = []
        for h in range(n_heads):
            pieces, row = [], 0
            for job in sorted(jobs):
                r0, r1, _ = job
                if r0 > row:
                    pieces.append(carries[h][row:r0])
                pieces.append(fresh[(job, h)])
                row = r1
            if row < tq:
                pieces.append(carries[h][row:tq])
            out.append(pieces[0] if len(pieces) == 1 else jnp.concatenate(pieces, axis=0))
        return tuple(out)

    zeros = jnp.zeros((tq, 1), F32)
    carries = block(i, (zeros,) * n_heads, [(0, tq, tk)], True)

    def least(carries):
        rows = functools.reduce(jnp.minimum, carries)
        below = jnp.min(rows[top:]) if top < tq else jnp.float32(SB_STICK_GONE_LOG2)
        return jnp.min(rows[:top]), below

    def walk(state, r1, watch):
        def live(state):
            return jnp.logical_and(state[0] < i, state[watch] < SB_STICK_GONE_LOG2)

        def body(state):
            n, _, _, carries = state
            carries = block(i - 1 - n, carries, [(0, r1, tk)], False)
            return (n + 1,) + least(carries) + (carries,)

        return lax.while_loop(live, body, state)

    state = (jnp.int32(0),) + least(carries) + (carries,)
    state = walk(state, tq, 2)
    walk(state, top, 1)
    for p in range(pairs):
        o_ref[:, p * LANES:(p + 1) * LANES] = jnp.where(
            lane < half, acc_ref[2 * p], acc_ref[2 * p + 1]).astype(o_ref.dtype)


def _sb_attention(sbq, sbkt, sbv, batch, seq):
    t = sbq.shape[0]
    tq, tk = min(SB_TQ, seq), min(SB_TK, seq)
    assert tq == tk
    nq = seq // tq
    pairs = SB_PAIRS_PER_STEP
    width = pairs * LANES
    groups = SB_WIDTH // width
    return pl.pallas_call(
        functools.partial(_sb_kernel, tq=tq, tk=tk, pairs=pairs),
        grid=(batch, groups, nq),
        in_specs=[
            pl.BlockSpec((tq, width), lambda b, p, i: (b * nq + i, p)),
            pl.BlockSpec((width, seq), lambda b, p, i: (p, b)),
            pl.BlockSpec((seq, width), lambda b, p, i: (b, p)),
        ],
        out_specs=pl.BlockSpec((tq, width), lambda b, p, i: (b * nq + i, p)),
        out_shape=jax.ShapeDtypeStruct((t, SB_WIDTH), BF16),
        scratch_shapes=[pltpu.VMEM((2 * pairs, tq, LANES), F32)],
        compiler_params=_params(("parallel", "parallel", "arbitrary")),
        name="sb_attention",
    )(sbq, sbkt, sbv)


def _mlstm_kernel(q_ref, kt_ref, v_ref, og_ref, grow_ref, brow_ref,
                  ghead_ref, o_ref, c_ref, m_ref, *, n_batch, n_chunks):
    L, d, H = ML_KERNEL_CHUNK, ML_HEAD_DIM, ML_HEADS
    chains = [(b, h) for b in range(n_batch) for h in range(H)]

    @pl.when(pl.program_id(1) == 0)
    def _():
        c_ref[...] = jnp.zeros_like(c_ref)
        m_ref[...] = jnp.zeros_like(m_ref)

    rr = lax.broadcasted_iota(jnp.int32, (L, L), 0)
    cc = lax.broadcasted_iota(jnp.int32, (L, L), 1)
    causal = cc <= rr
    ones = jnp.ones((L, d), BF16)
    gate_lane = lax.broadcasted_iota(jnp.int32, (2 * H, L), 1)

    def chunk(ci, ms):
        rows = pl.ds(pl.multiple_of(ci * L, L), L)
        lf_row, lf_scan, g_row = {}, {}, {}
        for b in range(n_batch):
            g_row[b] = grow_ref[b, :, rows] + brow_ref[...]
            lf_row[b] = _log_sigmoid(g_row[b])
            scan = lf_row[b]
            for k in range(L.bit_length() - 1):
                shifted = pltpu.roll(scan, 1 << k, axis=1)
                scan = scan + jnp.where(gate_lane >= (1 << k), shifted, 0.0)
            lf_scan[b] = scan

        qb, kt, v1, c_st, s_raw, q_c = {}, {}, {}, {}, {}, {}
        for ch in chains:
            b, h = ch
            cols = slice(h * d, (h + 1) * d)
            qb[ch] = q_ref[b, rows, cols]
            kt[ch] = kt_ref[b, cols, rows]
            v1[ch] = jnp.concatenate([v_ref[b, rows, cols], ones], axis=1)
            c_st[ch] = c_ref[b * H + h]
            s_raw[ch] = _dot(qb[ch], kt[ch])
            q_c[ch] = _dot(qb[ch], c_st[ch].astype(BF16))

        bcum_c, bcum_r, li_row = {}, {}, {}
        for ch in chains:
            b, h = ch
            li_row[ch] = g_row[b][h:h + 1, :]
            lf_r = lf_row[b][H + h:H + h + 1, :]
            bcum_c[ch] = jnp.sum(jnp.where(causal, lf_r, 0.0), axis=1, keepdims=True)
            bcum_r[ch] = lf_scan[b][H + h:H + h + 1, :]

        dmat, dmax = {}, {}
        for ch in chains:
            dmat[ch] = jnp.where(causal, bcum_c[ch] - bcum_r[ch] + li_row[ch], -jnp.inf)
            dmax[ch] = jnp.max(dmat[ch], axis=1, keepdims=True)

        w_inter, m_t, intra, w_state, decay, new_ms = {}, {}, {}, {}, {}, []
        for idx, ch in enumerate(chains):
            btot = bcum_c[ch][L - 1:L, :]
            m_prev = ms[idx]
            inter = bcum_c.pop(ch) + m_prev
            m_t[ch] = jnp.maximum(dmax.pop(ch), inter)
            w_intra = jnp.exp(dmat.pop(ch) - m_t[ch])
            w_inter[ch] = jnp.exp(inter - m_t[ch])
            sc = s_raw.pop(ch) * w_intra
            intra[ch] = _dot(sc.astype(BF16), v1[ch])
            m_new = m_t[ch][L - 1:L, :]
            w_state[ch] = jnp.exp(btot - bcum_r.pop(ch) + li_row.pop(ch) - m_new)
            decay[ch] = jnp.exp(btot + m_prev - m_new)
            new_ms.append(m_new)

        for ch in chains:
            b, h = ch
            ktw = (kt.pop(ch).astype(F32) * w_state.pop(ch)).astype(BF16)
            c_ref[b * H + h] = decay.pop(ch) * c_st.pop(ch) + _dot(ktw, v1.pop(ch))

        hval, ssq = {}, {}
        for ch in chains:
            both = intra.pop(ch) + w_inter.pop(ch) * q_c.pop(ch)
            den = both[:, d:2 * d]
            hval[ch] = both[:, 0:d] / jnp.maximum(jnp.abs(den), jnp.exp(-m_t.pop(ch)))
        for ch in chains:
            ssq[ch] = jnp.mean(hval[ch] * hval[ch], axis=-1, keepdims=True)
        for ch in chains:
            b, h = ch
            cols = slice(h * d, (h + 1) * d)
            hn = hval.pop(ch) * lax.rsqrt(ssq.pop(ch) + EPS) * ghead_ref[:, cols]
            og = og_ref[b, rows, cols]
            o_ref[b, rows, cols] = (hn * jax.nn.sigmoid(og)).astype(o_ref.dtype)
        return tuple(new_ms)

    ms = tuple(m_ref[i] for i in range(len(chains)))
    ms = lax.fori_loop(0, n_chunks, chunk, ms)
    for i, m in enumerate(ms):
        m_ref[i] = m


def _mlstm(mlq, mlkt, mlv, mlo, grow, brow, ghead):
    batch, seq, width = mlq.shape
    nb = min(ML_BATCH_PER_STEP, batch)
    sblk = min(ML_SEQ_BLOCK, seq)
    chunks = sblk // ML_KERNEL_CHUNK
    n_gate = 2 * ML_HEADS
    seq_blk = pl.BlockSpec((nb, sblk, width), lambda b, s: (b, s, 0))
    fixed2 = lambda b, s: (0, 0)
    return pl.pallas_call(
        functools.partial(_mlstm_kernel, n_batch=nb, n_chunks=chunks),
        grid=(batch // nb, seq // sblk),
        in_specs=[
            seq_blk,
            pl.BlockSpec((nb, width, sblk), lambda b, s: (b, 0, s)),
            seq_blk, seq_blk,
            pl.BlockSpec((nb, n_gate, sblk), lambda b, s: (b, 0, s)),
            pl.BlockSpec(brow.shape, fixed2),
            pl.BlockSpec(ghead.shape, fixed2),
        ],
        out_specs=seq_blk,
        out_shape=jax.ShapeDtypeStruct((batch, seq, width), BF16),
        scratch_shapes=[pltpu.VMEM((nb * ML_HEADS, ML_HEAD_DIM, 2 * ML_HEAD_DIM), F32),
                        pltpu.VMEM((nb * ML_HEADS, 1, 1), F32)],
        compiler_params=_params(("parallel", "arbitrary")),
        name="mlstm",
    )(mlq, mlkt, mlv, mlo, grow, brow, ghead)


def _memkv_kernel(mem_ref, gm_ref, wk_ref, wv_ref, gk_ref, k_ref, v_ref):
    mn = _rms(mem_ref[...], gm_ref[...]).astype(BF16)
    k = _dot(mn, wk_ref[...])
    hd = k.shape[1] // X_HEADS
    for h in range(X_HEADS):
        k_ref[:, h * hd:(h + 1) * hd] = _rms(k[:, h * hd:(h + 1) * hd], gk_ref[...]).astype(BF16)
    v_ref[...] = _dot(mn, wv_ref[...]).astype(BF16)


def _memkv(mem2d, gm, wk, wv, gk, batch, layer):
    rows, d = mem2d.shape
    m = rows // batch
    blk = pl.BlockSpec((m, d), lambda b: (b, 0))
    fixed = lambda b: (0, 0)
    return pl.pallas_call(
        _memkv_kernel,
        grid=(batch,),
        in_specs=[blk, pl.BlockSpec((1, d), fixed), _layer_weight((d, d), layer),
                  _layer_weight((d, d), layer), pl.BlockSpec((1, d // X_HEADS), fixed)],
        out_specs=[blk, blk],
        out_shape=[jax.ShapeDtypeStruct((rows, d), BF16)] * 2,
        compiler_params=_params(("parallel",)),
        name="memkv",
    )(mem2d, gm.reshape(1, d), wk, wv, gk.reshape(1, d // X_HEADS))


def _post_kernel(x_ref, sb_ref, ml_ref, wo_ref, gx_ref, wq_ref, gq_ref, k_ref, v_ref, wxo_ref,
                 o_ref):
    d = x_ref.shape[1]
    hd = d // X_HEADS
    x1 = (x_ref[...] + _dot(sb_ref[...], wo_ref[0:SB_WIDTH, :])
          + _dot(ml_ref[...], wo_ref[SB_WIDTH:SB_WIDTH + ML_WIDTH, :]))
    u = _rms(x1, gx_ref[...]).astype(BF16)
    q = _dot(u, wq_ref[...])
    outs = []
    for h in range(X_HEADS):
        cols = slice(h * hd, (h + 1) * hd)
        qn = _rms(q[:, cols], gq_ref[...]).astype(BF16)
        s = _dot_nt(qn, k_ref[:, cols]) * (hd ** -0.5)
        e = jnp.exp(s - jnp.max(s, axis=-1, keepdims=True))
        o = _dot(e.astype(BF16), v_ref[:, cols]) / jnp.sum(e, axis=-1, keepdims=True)
        outs.append(o.astype(BF16))
    o_ref[...] = x1 + _dot(jnp.concatenate(outs, axis=1), wxo_ref[...])


def _post(x, sb, ml, wo, gx, wxq, gq, kmem, vmem, wxo, seq, layer):
    t, d = x.shape
    tm = min(POST_TM, seq)
    per_seq = seq // tm
    m = kmem.shape[0] // (t // seq)
    row = lambda i: (i, 0)
    fixed = lambda i: (0, 0)
    mem_blk = pl.BlockSpec((m, d), lambda i: (i // per_seq, 0))
    return pl.pallas_call(
        _post_kernel,
        grid=(t // tm,),
        in_specs=[
            pl.BlockSpec((tm, d), row),
            pl.BlockSpec((tm, SB_WIDTH), row),
            pl.BlockSpec((tm, ML_WIDTH), row),
            _layer_weight((d, d), layer),
            pl.BlockSpec((1, d), fixed),
            _layer_weight((d, d), layer),
            pl.BlockSpec((1, d // X_HEADS), fixed),
            mem_blk, mem_blk,
            _layer_weight((d, d), layer),
        ],
        out_specs=pl.BlockSpec((tm, d), row),
        out_shape=jax.ShapeDtypeStruct((t, d), F32),
        compiler_params=_params(("parallel",)),
        name="post",
    )(x, sb, ml, wo, gx.reshape(1, d), wxq, gq.reshape(1, d // X_HEADS), kmem, vmem, wxo)


def _layer(x, mem2d, batch, seq, layer, p):
    l = layer
    x = _ffn(x, p["g_ff1"][l], p["w_ff1_gate"], p["w_ff1_up"], p["w_ff1_down"], l)

    s, w = SB_WIDTH, ML_WIDTH
    n_gate = 2 * ML_HEADS
    w_in = p["w_in"][l]
    sbq, sbkt, sbv, mlq, mlkt, mlv, mlo, gates = _proj(
        x, p["g_mix"][l], w_in[:, 0:s], w_in[:, s:2 * s].T, w_in[:, 2 * s:3 * s],
        w_in[:, 3 * s:3 * s + 4 * w], w_in[:, 3 * s + 4 * w:].T, p["w_conv"][l],
        p["b_conv"][l].reshape(1, 2 * w), seq)

    sb = _sb_attention(sbq, sbkt, sbv, batch, seq)

    seq3 = lambda a: a.reshape(batch, seq, w)
    ml = _mlstm(seq3(mlq), mlkt, seq3(mlv), seq3(mlo), gates, p["b_gate"][l].reshape(n_gate, 1),
                p["g_mlstm_head"][l].reshape(1, w)).reshape(batch * seq, w)

    kmem, vmem = _memkv(mem2d, p["g_mem"][l], p["w_xk"], p["w_xv"], p["g_knorm"][l], batch, l)
    x = _post(x, sb, ml, p["w_out"], p["g_xattn"][l], p["w_xq"], p["g_qnorm"][l], kmem, vmem,
              p["w_xo"], seq, l)
    x = _ffn(x, p["g_ff2"][l], p["w_ff2_gate"], p["w_ff2_up"], p["w_ff2_down"], l)
    return x


_MATMUL_WEIGHTS = ("w_ff1_gate", "w_ff1_up", "w_ff1_down", "w_in", "w_out", "w_xq", "w_xk", "w_xv",
                   "w_xo", "w_ff2_gate", "w_ff2_up", "w_ff2_down")


def kernel(x, mem, g_ff1, w_ff1_gate, w_ff1_up, w_ff1_down, g_mix, w_in, b_gate, w_conv, b_conv, g_mlstm_head, w_out, g_xattn, g_mem, w_xq, w_xk, w_xv, g_qnorm, g_knorm, w_xo, g_ff2, w_ff2_gate, w_ff2_up, w_ff2_down):
    batch, seq, d = x.shape
    depth = g_ff1.shape[0]
    p = dict(g_ff1=g_ff1, w_ff1_gate=w_ff1_gate, w_ff1_up=w_ff1_up, w_ff1_down=w_ff1_down,
             g_mix=g_mix, w_in=w_in, b_gate=b_gate, w_conv=w_conv, b_conv=b_conv,
             g_mlstm_head=g_mlstm_head, w_out=w_out, g_xattn=g_xattn, g_mem=g_mem, w_xq=w_xq,
             w_xk=w_xk, w_xv=w_xv, g_qnorm=g_qnorm, g_knorm=g_knorm, w_xo=w_xo, g_ff2=g_ff2,
             w_ff2_gate=w_ff2_gate, w_ff2_up=w_ff2_up, w_ff2_down=w_ff2_down)
    for name in _MATMUL_WEIGHTS:
        p[name] = p[name].astype(BF16)
    h = x.reshape(batch * seq, d)
    mem2d = mem.reshape(batch * mem.shape[1], d)
    for l in range(depth):
        h = _layer(h, mem2d, batch, seq, l, p)
    return h.reshape(batch, seq, d)
```

```python
import functools

import jax
import jax.numpy as jnp
from jax import lax
from jax.experimental import pallas as pl
from jax.experimental.pallas import tpu as pltpu

F32 = jnp.float32
BF16 = jnp.bfloat16

EPS = 1e-6
LOG2E = 1.4426950408889634
SB_HEADS = 8
SB_HEAD_DIM = 64
SB_WIDTH = SB_HEADS * SB_HEAD_DIM
ML_HEADS = 4
ML_HEAD_DIM = 128
ML_WIDTH = ML_HEADS * ML_HEAD_DIM
ML_KERNEL_CHUNK = 128
ML_STATE_PAD = 16
CONV_WIDTH = 4
X_HEADS = 4

V7X_VMEM_BYTES = 64 * 1024 * 1024
VMEM_LIMIT_BYTES = V7X_VMEM_BYTES - 8 * 1024 * 1024
LANES = 128
SUBLANES = 8

FFN_TM = 512
FFN_TF = 256
PROJ_TM = 512
POST_TM = 512
SB_TQ = 256
SB_TK = 256
SB_PAIRS_PER_STEP = 4
SB_STAGE_SKEW = 1
SB_STICK_GONE_LOG2 = 152.0
SB_TOP_ROWS = 176
ML_BATCH_PER_STEP = 4
ML_SEQ_BLOCK = 512


def _params(semantics):
    return pltpu.CompilerParams(dimension_semantics=semantics,
                                vmem_limit_bytes=VMEM_LIMIT_BYTES)


def _rms(x, g):
    return x * lax.rsqrt(jnp.mean(x * x, axis=-1, keepdims=True) + EPS) * g


def _dot(a, b):
    return jnp.dot(a, b, preferred_element_type=F32)


def _dot_nt(a, b):
    return lax.dot_general(a, b, (((1,), (1,)), ((), ())), preferred_element_type=F32)


def _log_sigmoid(x):
    return -(jnp.maximum(-x, 0.0) + jnp.log1p(jnp.exp(-jnp.abs(x))))


def _layer_weight(shape, layer):
    return pl.BlockSpec((None,) + shape, lambda *_: (layer, 0, 0), pipeline_mode=pl.Buffered(1))


def _ffn_kernel(x_ref, g_ref, wg_ref, wu_ref, wd_ref, o_ref, *, tf):
    x = x_ref[...]
    xn = _rms(x, g_ref[...]).astype(BF16)
    hidden = []
    for c in range(wg_ref.shape[1] // tf):
        cols = slice(c * tf, (c + 1) * tf)
        gate = _dot(xn, wg_ref[:, cols])
        up = _dot(xn, wu_ref[:, cols])
        hidden.append((gate * jax.nn.sigmoid(gate) * up).astype(BF16))
    o_ref[...] = x + 0.5 * _dot(jnp.concatenate(hidden, axis=1), wd_ref[...])


def _ffn(x, g, wg, wu, wd, layer):
    t, d = x.shape
    f = wg.shape[2]
    tm, tf = min(FFN_TM, t), min(FFN_TF, f)
    return pl.pallas_call(
        functools.partial(_ffn_kernel, tf=tf),
        grid=(t // tm,),
        in_specs=[
            pl.BlockSpec((tm, d), lambda i: (i, 0)),
            pl.BlockSpec((1, d), lambda i: (0, 0)),
            _layer_weight((d, f), layer), _layer_weight((d, f), layer),
            _layer_weight((f, d), layer),
        ],
        out_specs=pl.BlockSpec((tm, d), lambda i: (i, 0)),
        out_shape=jax.ShapeDtypeStruct((t, d), F32),
        compiler_params=_params(("parallel",)),
        name="ffn",
    )(x, g.reshape(1, d), wg, wu, wd)


def _proj_kernel(x_ref, xprev_ref, g_ref, wq_ref, wkt_ref, wv_ref, wml_ref, wmlvt_ref, wgate_ref,
                 wconv_ref, bconv_ref,
                 sbq_ref, sbkt_ref, sbv_ref, mlqt_ref, mlk_ref, mlvt_ref, mlo_ref, gates_ref,
                 *, tiles_per_seq):
    tm = x_ref.shape[0]
    w = ML_WIDTH
    u = _rms(x_ref[...], g_ref[...]).astype(BF16)
    u_prev = _rms(xprev_ref[...], g_ref[...]).astype(BF16)
    first = pl.program_id(0) % tiles_per_seq == 0

    def conv_silu(c0, c1):
        cur = _dot(u, wml_ref[:, c0:c1])
        prev = jnp.where(first, 0.0, _dot(u_prev, wml_ref[:, c0:c1]))
        ext = jnp.concatenate([prev, cur], axis=0)
        y = bconv_ref[:, c0:c1]
        for j in range(CONV_WIDTH):
            back = CONV_WIDTH - 1 - j
            window = pltpu.roll(ext, back, axis=0) if back else ext
            y = y + window[SUBLANES:, :] * wconv_ref[j:j + 1, c0:c1]
        return y * jax.nn.sigmoid(y)

    half = w // 2
    q_scale = ML_HEAD_DIM ** -0.5
    mlqt_ref[0:half, :] = (conv_silu(0, half) * q_scale).T.astype(BF16)
    sbq_ref[...] = (_dot(u, wq_ref[...]) * (LOG2E * SB_HEAD_DIM ** -0.5)).astype(BF16)
    mlqt_ref[half:w, :] = (conv_silu(half, w) * q_scale).T.astype(BF16)
    sbkt_ref[...] = _dot_nt(wkt_ref[...], u).astype(BF16)
    mlk_ref[:, 0:half] = conv_silu(w, w + half).astype(BF16)
    sbv_ref[...] = _dot(u, wv_ref[...]).astype(BF16)
    mlk_ref[:, half:w] = conv_silu(w + half, 2 * w).astype(BF16)
    mlvt_ref[...] = _dot_nt(wmlvt_ref[...], u).astype(BF16)
    mlo_ref[...] = _dot(u, wml_ref[:, 2 * w:3 * w])
    gates_ref[...] = _dot_nt(wgate_ref[...], u)


def _proj(x, g, wq, wkt, wv, wml, wmlvt, wgate, wconv, bconv, seq):
    t, d = x.shape
    tm = min(PROJ_TM, seq)
    tiles_per_seq = seq // tm
    row = lambda i: (i, 0)
    fixed = lambda i: (0, 0)
    whole = lambda a: pl.BlockSpec(a.shape, fixed, pipeline_mode=pl.Buffered(1))
    seq_t = lambda rows: pl.BlockSpec((None, rows, tm),
                                      lambda i: (i // tiles_per_seq, 0, i % tiles_per_seq))
    return pl.pallas_call(
        functools.partial(_proj_kernel, tiles_per_seq=tiles_per_seq),
        grid=(t // tm,),
        in_specs=[
            pl.BlockSpec((tm, d), row),
            pl.BlockSpec((SUBLANES, d), lambda i: (jnp.maximum(i * (tm // SUBLANES) - 1, 0), 0)),
            pl.BlockSpec((1, d), fixed),
            whole(wq), whole(wkt), whole(wv), whole(wml), whole(wmlvt), whole(wgate),
            whole(wconv), whole(bconv),
        ],
        out_specs=[
            pl.BlockSpec((tm, SB_WIDTH), row),
            pl.BlockSpec((SB_WIDTH, tm), lambda i: (0, i)),
            pl.BlockSpec((tm, SB_WIDTH), row),
            seq_t(ML_WIDTH),
            pl.BlockSpec((tm, ML_WIDTH), row),
            seq_t(ML_WIDTH),
            pl.BlockSpec((tm, ML_WIDTH), row),
            seq_t(2 * ML_HEADS),
        ],
        out_shape=[
            jax.ShapeDtypeStruct((t, SB_WIDTH), BF16),
            jax.ShapeDtypeStruct((SB_WIDTH, t), BF16),
            jax.ShapeDtypeStruct((t, SB_WIDTH), BF16),
            jax.ShapeDtypeStruct((t // seq, ML_WIDTH, seq), BF16),
            jax.ShapeDtypeStruct((t, ML_WIDTH), BF16),
            jax.ShapeDtypeStruct((t // seq, ML_WIDTH, seq), BF16),
            jax.ShapeDtypeStruct((t, ML_WIDTH), F32),
            jax.ShapeDtypeStruct((t // seq, 2 * ML_HEADS, seq), F32),
        ],
        compiler_params=_params(("parallel",)),
        name="proj",
    )(x, x, g.reshape(1, d), wq, wkt, wv, wml, wmlvt, wgate, wconv, bconv)


def _sb_kernel(q_ref, kt_ref, v_ref, o_ref, acc_ref, *, tq, tk, pairs):
    i = pl.program_id(2)
    half = SB_HEAD_DIM
    n_heads = 2 * pairs
    top = min(SB_TOP_ROWS, tq)
    lane = lax.broadcasted_iota(jnp.int32, (tq, LANES), 1)
    q_heads = []
    for p in range(pairs):
        q = q_ref[:, p * LANES:(p + 1) * LANES]
        zero = jnp.zeros_like(q)
        q_heads += [jnp.where(lane < half, q, zero), jnp.where(lane >= half, q, zero)]
    pair = lambda h: slice((h // 2) * LANES, (h // 2 + 1) * LANES)

    def suffix_ones(n):
        r = lax.broadcasted_iota(jnp.int32, (2 * n, n), 0)
        c = lax.broadcasted_iota(jnp.int32, (2 * n, n), 1)
        return jnp.where((r & (n - 1)) > c, 1.0, 0.0).astype(BF16)

    ones = {tk: suffix_ones(tk)}

    acc_ref[...] = jnp.zeros_like(acc_ref)

    def block(kb, carries, jobs, masked):
        start = pl.multiple_of(kb * tk, tk)
        chains = [(job, h) for job in jobs for h in range(n_heads)]
        z2, nl, later, fresh = {}, {}, {}, {}

        def strict(job):
            r0, r1, nk = job
            tr = lax.broadcasted_iota(jnp.int32, (r1 - r0, nk), 0) + r0
            tc = lax.broadcasted_iota(jnp.int32, (r1 - r0, nk), 1)
            return tc < tr

        def scores(ch):
            (r0, r1, nk), h = ch
            z2[ch] = _dot(q_heads[h][r0:r1], kt_ref[pair(h), pl.ds(start, nk)])

        def suffix_sums(ch):
            (r0, r1, nk), h = ch
            z = z2[ch]
            x = jnp.maximum(z, 0.0) + jnp.log(1.0 + jnp.exp2(-jnp.abs(z))) * LOG2E
            if masked:
                x = jnp.where(strict(ch[0]), x, 0.0)
            hi = x.astype(BF16)
            lo = (x - hi.astype(F32)).astype(BF16)
            nl[ch] = x
            later[ch] = _dot(jnp.concatenate([hi, lo], axis=1), ones[nk])

        def weights_times_values(ch):
            (r0, r1, nk), h = ch
            carry = carries[h][r0:r1]
            a = jnp.exp2(z2.pop(ch) - (nl[ch] + later[ch] + carry))
            if masked:
                a = jnp.where(strict(ch[0]), a, 0.0)
            acc_ref[h, r0:r1, :] += _dot(a.astype(BF16), v_ref[pl.ds(start, nk), pair(h)])
            fresh[ch] = carry + later.pop(ch)[:, 0:1] + nl.pop(ch)[:, 0:1]

        n = len(chains)
        for step in range(n + 2 * SB_STAGE_SKEW):
            if step < n:
                scores(chains[step])
            if 0 <= step - SB_STAGE_SKEW < n:
                suffix_sums(chains[step - SB_STAGE_SKEW])
            if 0 <= step - 2 * SB_STAGE_SKEW < n:
                weights_times_values(chains[step - 2 * SB_STAGE_SKEW])

        out = []
        for h in range(n_heads):
            pieces, row = [], 0
            for job in sorted(jobs):
                r0, r1, _ = job
                if r0 > row:
                    pieces.append(carries[h][row:r0])
                pieces.append(fresh[(job, h)])
                row = r1
            if row < tq:
                pieces.append(carries[h][row:tq])
            out.append(pieces[0] if len(pieces) == 1 else jnp.concatenate(pieces, axis=0))
        return tuple(out)

    zeros = jnp.zeros((tq, 1), F32)
    carries = block(i, (zeros,) * n_heads, [(0, tq, tk)], True)

    def least(carries):
        rows = functools.reduce(jnp.minimum, carries)
        below = jnp.min(rows[top:]) if top < tq else jnp.float32(SB_STICK_GONE_LOG2)
        return jnp.min(rows[:top]), below

    def walk(state, r1, watch):
        def live(state):
            return jnp.logical_and(state[0] < i, state[watch] < SB_STICK_GONE_LOG2)

        def body(state):
            n, _, _, carries = state
            carries = block(i - 1 - n, carries, [(0, r1, tk)], False)
            return (n + 1,) + least(carries) + (carries,)

        return lax.while_loop(live, body, state)

    state = (jnp.int32(0),) + least(carries) + (carries,)
    state = walk(state, tq, 2)
    walk(state, top, 1)
    for p in range(pairs):
        o_ref[:, p * LANES:(p + 1) * LANES] = jnp.where(
            lane < half, acc_ref[2 * p], acc_ref[2 * p + 1]).astype(o_ref.dtype)


def _sb_attention(sbq, sbkt, sbv, batch, seq):
    t = sbq.shape[0]
    tq, tk = min(SB_TQ, seq), min(SB_TK, seq)
    assert tq == tk
    nq = seq // tq
    pairs = SB_PAIRS_PER_STEP
    width = pairs * LANES
    groups = SB_WIDTH // width
    return pl.pallas_call(
        functools.partial(_sb_kernel, tq=tq, tk=tk, pairs=pairs),
        grid=(batch, groups, nq),
        in_specs=[
            pl.BlockSpec((tq, width), lambda b, p, i: (b * nq + i, p)),
            pl.BlockSpec((width, seq), lambda b, p, i: (p, b)),
            pl.BlockSpec((seq, width), lambda b, p, i: (b, p)),
        ],
        out_specs=pl.BlockSpec((tq, width), lambda b, p, i: (b * nq + i, p)),
        out_shape=jax.ShapeDtypeStruct((t, SB_WIDTH), BF16),
        scratch_shapes=[pltpu.VMEM((2 * pairs, tq, LANES), F32)],
        compiler_params=_params(("parallel", "parallel", "arbitrary")),
        name="sb_attention",
    )(sbq, sbkt, sbv)


def _mlstm_kernel(qt_ref, k_ref, vt_ref, og_ref, grow_ref, brow_ref,
                  ghead_ref, o_ref, c_ref, m_ref, gate_ref, scan_ref, *, n_batch, n_chunks):
    L, d, H = ML_KERNEL_CHUNK, ML_HEAD_DIM, ML_HEADS
    chains = [(b, h) for b in range(n_batch) for h in range(H)]

    @pl.when(pl.program_id(1) == 0)
    def _():
        c_ref[...] = jnp.zeros_like(c_ref)
        m_ref[...] = jnp.zeros_like(m_ref)

    ss = lax.broadcasted_iota(jnp.int32, (L, L), 0)
    tt = lax.broadcasted_iota(jnp.int32, (L, L), 1)
    causal = ss <= tt
    eye = ss == tt
    gate_lane = lax.broadcasted_iota(jnp.int32, (2 * H, L), 1)
    gain_col = [jnp.sum(jnp.where(eye, ghead_ref[:, h * d:(h + 1) * d], 0.0), axis=1, keepdims=True)
                for h in range(H)]

    for b in range(n_batch):
        for c in range(n_chunks):
            lanes = slice(c * L, (c + 1) * L)
            g = grow_ref[b, :, lanes] + brow_ref[...]
            scan = _log_sigmoid(g)
            for k in range(L.bit_length() - 1):
                shifted = pltpu.roll(scan, 1 << k, axis=1)
                scan = scan + jnp.where(gate_lane >= (1 << k), shifted, 0.0)
            gate_ref[b, :, lanes] = g
            scan_ref[b, :, lanes] = scan

    def chunk(ci, ms):
        rows = pl.ds(pl.multiple_of(ci * L, L), L)
        lf_scan, g_row = {}, {}
        for b in range(n_batch):
            g_row[b] = gate_ref[b, :, rows]
            lf_scan[b] = scan_ref[b, :, rows]

        kb, qt, vt, c_st, s_raw, q_c = {}, {}, {}, {}, {}, {}
        for ch in chains:
            b, h = ch
            cols = slice(h * d, (h + 1) * d)
            kb[ch] = k_ref[b, rows, cols]
            qt[ch] = qt_ref[b, cols, rows]
            vt[ch] = vt_ref[b, cols, rows]
            c_st[ch] = c_ref[b * H + h]
            s_raw[ch] = _dot(kb[ch], qt[ch])
            q_c[ch] = _dot(c_st[ch].astype(BF16), qt[ch])

        li_row, bcum, g_col = {}, {}, {}
        for ch in chains:
            b, h = ch
            li_row[ch] = g_row[b][h:h + 1, :]
            bcum[ch] = lf_scan[b][H + h:H + h + 1, :]
            g_col[ch] = jnp.sum(jnp.where(eye, li_row[ch] - bcum[ch], 0.0), axis=1, keepdims=True)

        w_inter, m_t, intra, den, w_state, decay, new_ms = {}, {}, {}, {}, {}, {}, []
        for idx, ch in enumerate(chains):
            m_prev = ms[idx]
            dmat = jnp.where(causal, bcum[ch] + g_col.pop(ch), -jnp.inf)
            inter = bcum[ch] + m_prev
            m_t[ch] = jnp.maximum(jnp.max(dmat, axis=0, keepdims=True), inter)
            sc = s_raw.pop(ch) * jnp.exp(dmat - m_t[ch])
            w_inter[ch] = jnp.exp(inter - m_t[ch])
            den[ch] = jnp.sum(sc, axis=0, keepdims=True)
            intra[ch] = _dot(vt[ch], sc.astype(BF16))
            btot = bcum[ch][:, L - 1:L]
            m_new = m_t[ch][:, L - 1:L]
            w_state[ch] = jnp.exp(btot - bcum.pop(ch) + li_row.pop(ch) - m_new)
            decay[ch] = jnp.exp(btot + m_prev - m_new)
            new_ms.append(m_new)

        for ch in chains:
            b, h = ch
            w = w_state.pop(ch)
            lhs = jnp.concatenate([vt.pop(ch).astype(F32) * w,
                                   jnp.broadcast_to(w, (ML_STATE_PAD, L))], axis=0)
            c_ref[b * H + h] = decay.pop(ch) * c_st.pop(ch) + _dot(lhs.astype(BF16), kb.pop(ch))

        for ch in chains:
            b, h = ch
            cols = slice(h * d, (h + 1) * d)
            readout = q_c.pop(ch)
            num = intra.pop(ch) + w_inter[ch] * readout[0:d]
            total = den.pop(ch) + w_inter.pop(ch) * readout[d:d + 1]
            hval = num / jnp.maximum(jnp.abs(total), jnp.exp(-m_t.pop(ch)))
            ssq = jnp.mean(hval * hval, axis=0, keepdims=True)
            hn = (hval * lax.rsqrt(ssq + EPS) * gain_col[h]).T
            og = og_ref[b, rows, cols]
            o_ref[b, rows, cols] = (hn * jax.nn.sigmoid(og)).astype(o_ref.dtype)
        return tuple(new_ms)

    ms = tuple(m_ref[i] for i in range(len(chains)))
    ms = lax.fori_loop(0, n_chunks, chunk, ms)
    for i, m in enumerate(ms):
        m_ref[i] = m


def _mlstm(mlqt, mlk, mlvt, mlo, grow, brow, ghead):
    batch, seq, width = mlk.shape
    nb = min(ML_BATCH_PER_STEP, batch)
    sblk = min(ML_SEQ_BLOCK, seq)
    chunks = sblk // ML_KERNEL_CHUNK
    n_gate = 2 * ML_HEADS
    seq_blk = pl.BlockSpec((nb, sblk, width), lambda b, s: (b, s, 0))
    seq_blk_t = pl.BlockSpec((nb, width, sblk), lambda b, s: (b, 0, s))
    fixed2 = lambda b, s: (0, 0)
    return pl.pallas_call(
        functools.partial(_mlstm_kernel, n_batch=nb, n_chunks=chunks),
        grid=(batch // nb, seq // sblk),
        in_specs=[
            seq_blk_t, seq_blk, seq_blk_t, seq_blk,
            pl.BlockSpec((nb, n_gate, sblk), lambda b, s: (b, 0, s)),
            pl.BlockSpec(brow.shape, fixed2),
            pl.BlockSpec(ghead.shape, fixed2),
        ],
        out_specs=seq_blk,
        out_shape=jax.ShapeDtypeStruct((batch, seq, width), BF16),
        scratch_shapes=[pltpu.VMEM((nb * ML_HEADS, ML_HEAD_DIM + ML_STATE_PAD, ML_HEAD_DIM), F32),
                        pltpu.VMEM((nb * ML_HEADS, 1, 1), F32),
                        pltpu.VMEM((nb, n_gate, sblk), F32),
                        pltpu.VMEM((nb, n_gate, sblk), F32)],
        compiler_params=_params(("parallel", "arbitrary")),
        name="mlstm",
    )(mlqt, mlk, mlvt, mlo, grow, brow, ghead)


def _memkv_kernel(mem_ref, gm_ref, wk_ref, wv_ref, gk_ref, k_ref, v_ref):
    mn = _rms(mem_ref[...], gm_ref[...]).astype(BF16)
    k = _dot(mn, wk_ref[...])
    hd = k.shape[1] // X_HEADS
    for h in range(X_HEADS):
        k_ref[:, h * hd:(h + 1) * hd] = _rms(k[:, h * hd:(h + 1) * hd], gk_ref[...]).astype(BF16)
    v_ref[...] = _dot(mn, wv_ref[...]).astype(BF16)


def _memkv(mem2d, gm, wk, wv, gk, batch, layer):
    rows, d = mem2d.shape
    m = rows // batch
    blk = pl.BlockSpec((m, d), lambda b: (b, 0))
    fixed = lambda b: (0, 0)
    return pl.pallas_call(
        _memkv_kernel,
        grid=(batch,),
        in_specs=[blk, pl.BlockSpec((1, d), fixed), _layer_weight((d, d), layer),
                  _layer_weight((d, d), layer), pl.BlockSpec((1, d // X_HEADS), fixed)],
        out_specs=[blk, blk],
        out_shape=[jax.ShapeDtypeStruct((rows, d), BF16)] * 2,
        compiler_params=_params(("parallel",)),
        name="memkv",
    )(mem2d, gm.reshape(1, d), wk, wv, gk.reshape(1, d // X_HEADS))


def _post_kernel(x_ref, sb_ref, ml_ref, wo_ref, gx_ref, wq_ref, gq_ref, k_ref, v_ref, wxo_ref,
                 o_ref):
    d = x_ref.shape[1]
    hd = d // X_HEADS
    x1 = (x_ref[...] + _dot(sb_ref[...], wo_ref[0:SB_WIDTH, :])
          + _dot(ml_ref[...], wo_ref[SB_WIDTH:SB_WIDTH + ML_WIDTH, :]))
    u = _rms(x1, gx_ref[...]).astype(BF16)
    q = _dot(u, wq_ref[...])
    outs = []
    for h in range(X_HEADS):
        cols = slice(h * hd, (h + 1) * hd)
        qn = _rms(q[:, cols], gq_ref[...]).astype(BF16)
        s = _dot_nt(qn, k_ref[:, cols]) * (hd ** -0.5)
        e = jnp.exp(s - jnp.max(s, axis=-1, keepdims=True))
        o = _dot(e.astype(BF16), v_ref[:, cols]) / jnp.sum(e, axis=-1, keepdims=True)
        outs.append(o.astype(BF16))
    o_ref[...] = x1 + _dot(jnp.concatenate(outs, axis=1), wxo_ref[...])


def _post(x, sb, ml, wo, gx, wxq, gq, kmem, vmem, wxo, seq, layer):
    t, d = x.shape
    tm = min(POST_TM, seq)
    per_seq = seq // tm
    m = kmem.shape[0] // (t // seq)
    row = lambda i: (i, 0)
    fixed = lambda i: (0, 0)
    mem_blk = pl.BlockSpec((m, d), lambda i: (i // per_seq, 0))
    return pl.pallas_call(
        _post_kernel,
        grid=(t // tm,),
        in_specs=[
            pl.BlockSpec((tm, d), row),
            pl.BlockSpec((tm, SB_WIDTH), row),
            pl.BlockSpec((tm, ML_WIDTH), row),
            _layer_weight((d, d), layer),
            pl.BlockSpec((1, d), fixed),
            _layer_weight((d, d), layer),
            pl.BlockSpec((1, d // X_HEADS), fixed),
            mem_blk, mem_blk,
            _layer_weight((d, d), layer),
        ],
        out_specs=pl.BlockSpec((tm, d), row),
        out_shape=jax.ShapeDtypeStruct((t, d), F32),
        compiler_params=_params(("parallel",)),
        name="post",
    )(x, sb, ml, wo, gx.reshape(1, d), wxq, gq.reshape(1, d // X_HEADS), kmem, vmem, wxo)


def _layer(x, mem2d, batch, seq, layer, p):
    l = layer
    x = _ffn(x, p["g_ff1"][l], p["w_ff1_gate"], p["w_ff1_up"], p["w_ff1_down"], l)

    s, w = SB_WIDTH, ML_WIDTH
    n_gate = 2 * ML_HEADS
    w_in = p["w_in"][l]
    ml0 = 3 * s
    w_ml_qko = jnp.concatenate([w_in[:, ml0:ml0 + 2 * w], w_in[:, ml0 + 3 * w:ml0 + 4 * w]], axis=1)
    sbq, sbkt, sbv, mlqt, mlk, mlvt, mlo, gates = _proj(
        x, p["g_mix"][l], w_in[:, 0:s], w_in[:, s:2 * s].T, w_in[:, 2 * s:3 * s],
        w_ml_qko, w_in[:, ml0 + 2 * w:ml0 + 3 * w].T, w_in[:, ml0 + 4 * w:].T, p["w_conv"][l],
        p["b_conv"][l].reshape(1, 2 * w), seq)

    sb = _sb_attention(sbq, sbkt, sbv, batch, seq)

    seq3 = lambda a: a.reshape(batch, seq, w)
    ml = _mlstm(mlqt, seq3(mlk), mlvt, seq3(mlo), gates, p["b_gate"][l].reshape(n_gate, 1),
                p["g_mlstm_head"][l].reshape(1, w)).reshape(batch * seq, w)

    kmem, vmem = _memkv(mem2d, p["g_mem"][l], p["w_xk"], p["w_xv"], p["g_knorm"][l], batch, l)
    x = _post(x, sb, ml, p["w_out"], p["g_xattn"][l], p["w_xq"], p["g_qnorm"][l], kmem, vmem,
              p["w_xo"], seq, l)
    x = _ffn(x, p["g_ff2"][l], p["w_ff2_gate"], p["w_ff2_up"], p["w_ff2_down"], l)
    return x


_MATMUL_WEIGHTS = ("w_ff1_gate", "w_ff1_up", "w_ff1_down", "w_in", "w_out", "w_xq", "w_xk", "w_xv",
                   "w_xo", "w_ff2_gate", "w_ff2_up", "w_ff2_down")


def kernel(x, mem, g_ff1, w_ff1_gate, w_ff1_up, w_ff1_down, g_mix, w_in, b_gate, w_conv, b_conv, g_mlstm_head, w_out, g_xattn, g_mem, w_xq, w_xk, w_xv, g_qnorm, g_knorm, w_xo, g_ff2, w_ff2_gate, w_ff2_up, w_ff2_down):
    batch, seq, d = x.shape
    depth = g_ff1.shape[0]
    p = dict(g_ff1=g_ff1, w_ff1_gate=w_ff1_gate, w_ff1_up=w_ff1_up, w_ff1_down=w_ff1_down,
             g_mix=g_mix, w_in=w_in, b_gate=b_gate, w_conv=w_conv, b_conv=b_conv,
             g_mlstm_head=g_mlstm_head, w_out=w_out, g_xattn=g_xattn, g_mem=g_mem, w_xq=w_xq,
             w_xk=w_xk, w_xv=w_xv, g_qnorm=g_qnorm, g_knorm=g_knorm, w_xo=w_xo, g_ff2=g_ff2,
             w_ff2_gate=w_ff2_gate, w_ff2_up=w_ff2_up, w_ff2_down=w_ff2_down)
    for name in _MATMUL_WEIGHTS:
        p[name] = p[name].astype(BF16)
    h = x.reshape(batch * seq, d)
    mem2d = mem.reshape(batch * mem.shape[1], d)
    for l in range(depth):
        h = _layer(h, mem2d, batch, seq, l, p)
    return h.reshape(batch, seq, d)
```

```python
import functools

import jax
import jax.numpy as jnp
from jax import lax
from jax.experimental import pallas as pl
from jax.experimental.pallas import tpu as pltpu

F32 = jnp.float32
BF16 = jnp.bfloat16

EPS = 1e-6
LOG2E = 1.4426950408889634
SB_HEADS = 8
SB_HEAD_DIM = 64
SB_WIDTH = SB_HEADS * SB_HEAD_DIM
ML_HEADS = 4
ML_HEAD_DIM = 128
ML_WIDTH = ML_HEADS * ML_HEAD_DIM
ML_KERNEL_CHUNK = 128
ML_STATE_PAD = 16
CONV_WIDTH = 4
X_HEADS = 4

V7X_VMEM_BYTES = 64 * 1024 * 1024
VMEM_LIMIT_BYTES = V7X_VMEM_BYTES - 8 * 1024 * 1024
LANES = 128
SUBLANES = 8

FFN_TM = 1024
FFN_TF = 256
PROJ_TM = 512
POST_TM = 1024
POST_ROW_PARTS = 4
SB_TQ = 256
SB_TK = 256
SB_PAIRS_PER_STEP = 4
SB_STAGE_SKEW = 1
SB_STICK_GONE_LOG2 = 152.0
SB_MASKED_LOGIT = -1e30
SB_TOP_ROWS = 176
ML_BATCH_PER_STEP = 4
ML_SEQ_BLOCK = 512


def _params(semantics):
    return pltpu.CompilerParams(dimension_semantics=semantics,
                                vmem_limit_bytes=VMEM_LIMIT_BYTES)


def _rms(x, g):
    return x * lax.rsqrt(jnp.mean(x * x, axis=-1, keepdims=True) + EPS) * g


def _dot(a, b):
    return jnp.dot(a, b, preferred_element_type=F32)


def _dot_nt(a, b):
    return lax.dot_general(a, b, (((1,), (1,)), ((), ())), preferred_element_type=F32)


def _log_sigmoid(x):
    return -(jnp.maximum(-x, 0.0) + jnp.log1p(jnp.exp(-jnp.abs(x))))


def _layer_weight(shape, layer):
    return pl.BlockSpec((None,) + shape, lambda *_: (layer, 0, 0), pipeline_mode=pl.Buffered(1))


def _ffn_kernel(x_ref, g_ref, wg_ref, wu_ref, wd_ref, o_ref, *, tf):
    x = x_ref[...]
    xn = _rms(x, g_ref[...]).astype(BF16)
    hidden = []
    for c in range(wg_ref.shape[1] // tf):
        cols = slice(c * tf, (c + 1) * tf)
        gate = _dot(xn, wg_ref[:, cols])
        up = _dot(xn, wu_ref[:, cols])
        hidden.append((gate * jax.nn.sigmoid(gate) * up).astype(BF16))
    o_ref[...] = x + 0.5 * _dot(jnp.concatenate(hidden, axis=1), wd_ref[...])


def _ffn(x, g, wg, wu, wd, layer):
    t, d = x.shape
    f = wg.shape[2]
    tm, tf = min(FFN_TM, t), min(FFN_TF, f)
    return pl.pallas_call(
        functools.partial(_ffn_kernel, tf=tf),
        grid=(t // tm,),
        in_specs=[
            pl.BlockSpec((tm, d), lambda i: (i, 0)),
            pl.BlockSpec((1, d), lambda i: (0, 0)),
            _layer_weight((d, f), layer), _layer_weight((d, f), layer),
            _layer_weight((f, d), layer),
        ],
        out_specs=pl.BlockSpec((tm, d), lambda i: (i, 0)),
        out_shape=jax.ShapeDtypeStruct((t, d), F32),
        compiler_params=_params(("parallel",)),
        name="ffn",
    )(x, g.reshape(1, d), wg, wu, wd)


def _proj_kernel(x_ref, xprev_ref, g_ref, wq_ref, wkt_ref, wv_ref, wml_ref, wmlvt_ref, wgate_ref,
                 wconv_ref, bconv_ref,
                 sbq_ref, sbkt_ref, sbv_ref, mlqt_ref, mlk_ref, mlvt_ref, mlo_ref, gates_ref,
                 *, tiles_per_seq):
    tm = x_ref.shape[0]
    w = ML_WIDTH
    u = _rms(x_ref[...], g_ref[...]).astype(BF16)
    u_prev = _rms(xprev_ref[...], g_ref[...]).astype(BF16)
    first = pl.program_id(0) % tiles_per_seq == 0

    def conv_silu(c0, c1):
        cur = _dot(u, wml_ref[:, c0:c1])
        prev = jnp.where(first, 0.0, _dot(u_prev, wml_ref[:, c0:c1]))
        ext = jnp.concatenate([prev, cur], axis=0)
        y = bconv_ref[:, c0:c1]
        for j in range(CONV_WIDTH):
            back = CONV_WIDTH - 1 - j
            window = pltpu.roll(ext, back, axis=0) if back else ext
            y = y + window[SUBLANES:, :] * wconv_ref[j:j + 1, c0:c1]
        return y * jax.nn.sigmoid(y)

    half = w // 2
    q_scale = ML_HEAD_DIM ** -0.5
    mlqt_ref[0:half, :] = (conv_silu(0, half) * q_scale).T.astype(BF16)
    sbq_ref[...] = (_dot(u, wq_ref[...]) * (LOG2E * SB_HEAD_DIM ** -0.5)).astype(BF16)
    mlqt_ref[half:w, :] = (conv_silu(half, w) * q_scale).T.astype(BF16)
    sbkt_ref[...] = _dot_nt(wkt_ref[...], u).astype(BF16)
    mlk_ref[:, 0:half] = conv_silu(w, w + half).astype(BF16)
    sbv_ref[...] = _dot(u, wv_ref[...]).astype(BF16)
    mlk_ref[:, half:w] = conv_silu(w + half, 2 * w).astype(BF16)
    mlvt_ref[...] = _dot_nt(wmlvt_ref[...], u).astype(BF16)
    mlo_ref[...] = _dot(u, wml_ref[:, 2 * w:3 * w])
    gates_ref[...] = _dot_nt(wgate_ref[...], u)


def _proj(x, g, wq, wkt, wv, wml, wmlvt, wgate, wconv, bconv, seq):
    t, d = x.shape
    tm = min(PROJ_TM, seq)
    tiles_per_seq = seq // tm
    row = lambda i: (i, 0)
    fixed = lambda i: (0, 0)
    whole = lambda a: pl.BlockSpec(a.shape, fixed, pipeline_mode=pl.Buffered(1))
    seq_t = lambda rows: pl.BlockSpec((None, rows, tm),
                                      lambda i: (i // tiles_per_seq, 0, i % tiles_per_seq))
    return pl.pallas_call(
        functools.partial(_proj_kernel, tiles_per_seq=tiles_per_seq),
        grid=(t // tm,),
        in_specs=[
            pl.BlockSpec((tm, d), row),
            pl.BlockSpec((SUBLANES, d), lambda i: (jnp.maximum(i * (tm // SUBLANES) - 1, 0), 0)),
            pl.BlockSpec((1, d), fixed),
            whole(wq), whole(wkt), whole(wv), whole(wml), whole(wmlvt), whole(wgate),
            whole(wconv), whole(bconv),
        ],
        out_specs=[
            pl.BlockSpec((tm, SB_WIDTH), row),
            pl.BlockSpec((SB_WIDTH, tm), lambda i: (0, i)),
            pl.BlockSpec((tm, SB_WIDTH), row),
            seq_t(ML_WIDTH),
            pl.BlockSpec((tm, ML_WIDTH), row),
            seq_t(ML_WIDTH),
            pl.BlockSpec((tm, ML_WIDTH), row),
            seq_t(2 * ML_HEADS),
        ],
        out_shape=[
            jax.ShapeDtypeStruct((t, SB_WIDTH), BF16),
            jax.ShapeDtypeStruct((SB_WIDTH, t), BF16),
            jax.ShapeDtypeStruct((t, SB_WIDTH), BF16),
            jax.ShapeDtypeStruct((t // seq, ML_WIDTH, seq), BF16),
            jax.ShapeDtypeStruct((t, ML_WIDTH), BF16),
            jax.ShapeDtypeStruct((t // seq, ML_WIDTH, seq), BF16),
            jax.ShapeDtypeStruct((t, ML_WIDTH), F32),
            jax.ShapeDtypeStruct((t // seq, 2 * ML_HEADS, seq), F32),
        ],
        compiler_params=_params(("parallel",)),
        name="proj",
    )(x, x, g.reshape(1, d), wq, wkt, wv, wml, wmlvt, wgate, wconv, bconv)


def _sb_kernel(q_ref, kt_ref, v_ref, o_ref, acc_ref, *, tq, tk, pairs):
    i = pl.program_id(2)
    half = SB_HEAD_DIM
    n_heads = 2 * pairs
    top = min(SB_TOP_ROWS, tq)
    lane = lax.broadcasted_iota(jnp.int32, (tq, LANES), 1)
    q_heads = []
    for p in range(pairs):
        q = q_ref[:, p * LANES:(p + 1) * LANES]
        zero = jnp.zeros_like(q)
        q_heads += [jnp.where(lane < half, q, zero), jnp.where(lane >= half, q, zero)]
    pair = lambda h: slice((h // 2) * LANES, (h // 2 + 1) * LANES)

    def suffix_ones(n):
        r = lax.broadcasted_iota(jnp.int32, (2 * n, n), 0)
        c = lax.broadcasted_iota(jnp.int32, (2 * n, n), 1)
        return jnp.where((r & (n - 1)) > c, 1.0, 0.0).astype(BF16)

    ones = {tk: suffix_ones(tk)}

    acc_ref[...] = jnp.zeros_like(acc_ref)

    def block(kb, carries, jobs, masked):
        start = pl.multiple_of(kb * tk, tk)
        chains = [(job, h) for job in jobs for h in range(n_heads)]
        z2, nl, later, fresh = {}, {}, {}, {}

        def strict(job):
            r0, r1, nk = job
            tr = lax.broadcasted_iota(jnp.int32, (r1 - r0, nk), 0) + r0
            tc = lax.broadcasted_iota(jnp.int32, (r1 - r0, nk), 1)
            return tc < tr

        def scores(ch):
            (r0, r1, nk), h = ch
            z = _dot(q_heads[h][r0:r1], kt_ref[pair(h), pl.ds(start, nk)])
            if masked:
                z = jnp.where(strict(ch[0]), z, SB_MASKED_LOGIT)
            z2[ch] = z

        def suffix_sums(ch):
            (r0, r1, nk), h = ch
            z = z2[ch]
            x = jnp.maximum(z, 0.0) + jnp.log(1.0 + jnp.exp2(-jnp.abs(z))) * LOG2E
            hi = x.astype(BF16)
            lo = (x - hi.astype(F32)).astype(BF16)
            nl[ch] = x
            later[ch] = _dot(jnp.concatenate([hi, lo], axis=1), ones[nk])

        def weights_times_values(ch):
            (r0, r1, nk), h = ch
            carry = carries[h][r0:r1]
            a = jnp.exp2(z2.pop(ch) - (nl[ch] + later[ch] + carry))
            acc_ref[h, r0:r1, :] += _dot(a.astype(BF16), v_ref[pl.ds(start, nk), pair(h)])
            fresh[ch] = carry + later.pop(ch)[:, 0:1] + nl.pop(ch)[:, 0:1]

        n = len(chains)
        for step in range(n + 2 * SB_STAGE_SKEW):
            if step < n:
                scores(chains[step])
            if 0 <= step - SB_STAGE_SKEW < n:
                suffix_sums(chains[step - SB_STAGE_SKEW])
            if 0 <= step - 2 * SB_STAGE_SKEW < n:
                weights_times_values(chains[step - 2 * SB_STAGE_SKEW])

        out = []
        for h in range(n_heads):
            pieces, row = [], 0
            for job in sorted(jobs):
                r0, r1, _ = job
                if r0 > row:
                    pieces.append(carries[h][row:r0])
                pieces.append(fresh[(job, h)])
                row = r1
            if row < tq:
                pieces.append(carries[h][row:tq])
            out.append(pieces[0] if len(pieces) == 1 else jnp.concatenate(pieces, axis=0))
        return tuple(out)

    zeros = jnp.zeros((tq, 1), F32)
    carries = block(i, (zeros,) * n_heads, [(0, tq, tk)], True)

    def least(carries):
        rows = functools.reduce(jnp.minimum, carries)
        below = jnp.min(rows[top:]) if top < tq else jnp.float32(SB_STICK_GONE_LOG2)
        return jnp.min(rows[:top]), below

    def walk(state, r1, watch):
        def live(state):
            return jnp.logical_and(state[0] < i, state[watch] < SB_STICK_GONE_LOG2)

        def body(state):
            n, _, _, carries = state
            carries = block(i - 1 - n, carries, [(0, r1, tk)], False)
            return (n + 1,) + least(carries) + (carries,)

        return lax.while_loop(live, body, state)

    state = (jnp.int32(0),) + least(carries) + (carries,)
    state = walk(state, tq, 2)
    walk(state, top, 1)
    for p in range(pairs):
        o_ref[:, p * LANES:(p + 1) * LANES] = jnp.where(
            lane < half, acc_ref[2 * p], acc_ref[2 * p + 1]).astype(o_ref.dtype)


def _sb_attention(sbq, sbkt, sbv, batch, seq):
    t = sbq.shape[0]
    tq, tk = min(SB_TQ, seq), min(SB_TK, seq)
    assert tq == tk
    nq = seq // tq
    pairs = SB_PAIRS_PER_STEP
    width = pairs * LANES
    groups = SB_WIDTH // width
    return pl.pallas_call(
        functools.partial(_sb_kernel, tq=tq, tk=tk, pairs=pairs),
        grid=(batch, groups, nq),
        in_specs=[
            pl.BlockSpec((tq, width), lambda b, p, i: (b * nq + i, p)),
            pl.BlockSpec((width, seq), lambda b, p, i: (p, b)),
            pl.BlockSpec((seq, width), lambda b, p, i: (b, p)),
        ],
        out_specs=pl.BlockSpec((tq, width), lambda b, p, i: (b * nq + i, p)),
        out_shape=jax.ShapeDtypeStruct((t, SB_WIDTH), BF16),
        scratch_shapes=[pltpu.VMEM((2 * pairs, tq, LANES), F32)],
        compiler_params=_params(("parallel", "parallel", "arbitrary")),
        name="sb_attention",
    )(sbq, sbkt, sbv)


def _mlstm_kernel(qt_ref, k_ref, vt_ref, og_ref, grow_ref, brow_ref,
                  ghead_ref, o_ref, c_ref, m_ref, gate_ref, scan_ref, *, n_batch, n_chunks):
    L, d, H = ML_KERNEL_CHUNK, ML_HEAD_DIM, ML_HEADS
    chains = [(b, h) for b in range(n_batch) for h in range(H)]

    @pl.when(pl.program_id(1) == 0)
    def _():
        c_ref[...] = jnp.zeros_like(c_ref)
        m_ref[...] = jnp.zeros_like(m_ref)

    ss = lax.broadcasted_iota(jnp.int32, (L, L), 0)
    tt = lax.broadcasted_iota(jnp.int32, (L, L), 1)
    causal = ss <= tt
    eye = ss == tt
    gate_lane = lax.broadcasted_iota(jnp.int32, (2 * H, L), 1)
    gain_col = [jnp.sum(jnp.where(eye, ghead_ref[:, h * d:(h + 1) * d], 0.0), axis=1, keepdims=True)
                for h in range(H)]

    for b in range(n_batch):
        for c in range(n_chunks):
            lanes = slice(c * L, (c + 1) * L)
            g = grow_ref[b, :, lanes] + brow_ref[...]
            scan = _log_sigmoid(g)
            for k in range(L.bit_length() - 1):
                shifted = pltpu.roll(scan, 1 << k, axis=1)
                scan = scan + jnp.where(gate_lane >= (1 << k), shifted, 0.0)
            gate_ref[b, :, lanes] = g
            scan_ref[b, :, lanes] = scan

    def chunk(ci, ms):
        rows = pl.ds(pl.multiple_of(ci * L, L), L)
        lf_scan, g_row = {}, {}
        for b in range(n_batch):
            g_row[b] = gate_ref[b, :, rows]
            lf_scan[b] = scan_ref[b, :, rows]

        kb, qt, vt, c_st, s_raw, q_c = {}, {}, {}, {}, {}, {}
        for ch in chains:
            b, h = ch
            cols = slice(h * d, (h + 1) * d)
            kb[ch] = k_ref[b, rows, cols]
            qt[ch] = qt_ref[b, cols, rows]
            vt[ch] = vt_ref[b, cols, rows]
            c_st[ch] = c_ref[b * H + h]
            s_raw[ch] = _dot(kb[ch], qt[ch])
            q_c[ch] = _dot(c_st[ch].astype(BF16), qt[ch])

        li_row, bcum, g_col = {}, {}, {}
        for ch in chains:
            b, h = ch
            li_row[ch] = g_row[b][h:h + 1, :]
            bcum[ch] = lf_scan[b][H + h:H + h + 1, :]
            g_col[ch] = jnp.sum(jnp.where(eye, li_row[ch] - bcum[ch], 0.0), axis=1, keepdims=True)

        w_inter, m_t, intra, den, w_state, decay, new_ms = {}, {}, {}, {}, {}, {}, []
        for idx, ch in enumerate(chains):
            m_prev = ms[idx]
            dmat = jnp.where(causal, bcum[ch] + g_col.pop(ch), -jnp.inf)
            inter = bcum[ch] + m_prev
            m_t[ch] = jnp.maximum(jnp.max(dmat, axis=0, keepdims=True), inter)
            sc = s_raw.pop(ch) * jnp.exp(dmat - m_t[ch])
            w_inter[ch] = jnp.exp(inter - m_t[ch])
            den[ch] = jnp.sum(sc, axis=0, keepdims=True)
            intra[ch] = _dot(vt[ch], sc.astype(BF16))
            btot = bcum[ch][:, L - 1:L]
            m_new = m_t[ch][:, L - 1:L]
            w_state[ch] = jnp.exp(btot - bcum.pop(ch) + li_row.pop(ch) - m_new)
            decay[ch] = jnp.exp(btot + m_prev - m_new)
            new_ms.append(m_new)

        for ch in chains:
            b, h = ch
            w = w_state.pop(ch)
            lhs = jnp.concatenate([vt.pop(ch).astype(F32) * w,
                                   jnp.broadcast_to(w, (ML_STATE_PAD, L))], axis=0)
            c_ref[b * H + h] = decay.pop(ch) * c_st.pop(ch) + _dot(lhs.astype(BF16), kb.pop(ch))

        for ch in chains:
            b, h = ch
            cols = slice(h * d, (h + 1) * d)
            readout = q_c.pop(ch)
            num = intra.pop(ch) + w_inter[ch] * readout[0:d]
            total = den.pop(ch) + w_inter.pop(ch) * readout[d:d + 1]
            hval = num / jnp.maximum(jnp.abs(total), jnp.exp(-m_t.pop(ch)))
            ssq = jnp.mean(hval * hval, axis=0, keepdims=True)
            hn = (hval * lax.rsqrt(ssq + EPS) * gain_col[h]).T
            og = og_ref[b, rows, cols]
            o_ref[b, rows, cols] = (hn * jax.nn.sigmoid(og)).astype(o_ref.dtype)
        return tuple(new_ms)

    ms = tuple(m_ref[i] for i in range(len(chains)))
    ms = lax.fori_loop(0, n_chunks, chunk, ms)
    for i, m in enumerate(ms):
        m_ref[i] = m


def _mlstm(mlqt, mlk, mlvt, mlo, grow, brow, ghead):
    batch, seq, width = mlk.shape
    nb = min(ML_BATCH_PER_STEP, batch)
    sblk = min(ML_SEQ_BLOCK, seq)
    chunks = sblk // ML_KERNEL_CHUNK
    n_gate = 2 * ML_HEADS
    seq_blk = pl.BlockSpec((nb, sblk, width), lambda b, s: (b, s, 0))
    seq_blk_t = pl.BlockSpec((nb, width, sblk), lambda b, s: (b, 0, s))
    fixed2 = lambda b, s: (0, 0)
    return pl.pallas_call(
        functools.partial(_mlstm_kernel, n_batch=nb, n_chunks=chunks),
        grid=(batch // nb, seq // sblk),
        in_specs=[
            seq_blk_t, seq_blk, seq_blk_t, seq_blk,
            pl.BlockSpec((nb, n_gate, sblk), lambda b, s: (b, 0, s)),
            pl.BlockSpec(brow.shape, fixed2),
            pl.BlockSpec(ghead.shape, fixed2),
        ],
        out_specs=seq_blk,
        out_shape=jax.ShapeDtypeStruct((batch, seq, width), BF16),
        scratch_shapes=[pltpu.VMEM((nb * ML_HEADS, ML_HEAD_DIM + ML_STATE_PAD, ML_HEAD_DIM), F32),
                        pltpu.VMEM((nb * ML_HEADS, 1, 1), F32),
                        pltpu.VMEM((nb, n_gate, sblk), F32),
                        pltpu.VMEM((nb, n_gate, sblk), F32)],
        compiler_params=_params(("parallel", "arbitrary")),
        name="mlstm",
    )(mlqt, mlk, mlvt, mlo, grow, brow, ghead)


def _memkv_kernel(mem_ref, gm_ref, wk_ref, wv_ref, gk_ref, k_ref, v_ref):
    mn = _rms(mem_ref[...], gm_ref[...]).astype(BF16)
    k = _dot(mn, wk_ref[...])
    hd = k.shape[1] // X_HEADS
    for h in range(X_HEADS):
        k_ref[:, h * hd:(h + 1) * hd] = _rms(k[:, h * hd:(h + 1) * hd], gk_ref[...]).astype(BF16)
    v_ref[...] = _dot(mn, wv_ref[...]).astype(BF16)


def _memkv(mem2d, gm, wk, wv, gk, batch, layer):
    rows, d = mem2d.shape
    m = rows // batch
    blk = pl.BlockSpec((m, d), lambda b: (b, 0))
    fixed = lambda b: (0, 0)
    return pl.pallas_call(
        _memkv_kernel,
        grid=(batch,),
        in_specs=[blk, pl.BlockSpec((1, d), fixed), _layer_weight((d, d), layer),
                  _layer_weight((d, d), layer), pl.BlockSpec((1, d // X_HEADS), fixed)],
        out_specs=[blk, blk],
        out_shape=[jax.ShapeDtypeStruct((rows, d), BF16)] * 2,
        compiler_params=_params(("parallel",)),
        name="memkv",
    )(mem2d, gm.reshape(1, d), wk, wv, gk.reshape(1, d // X_HEADS))


def _post_kernel(x_ref, sb_ref, ml_ref, wo_ref, gx_ref, wq_ref, gq_ref, k_ref, v_ref, wxo_ref,
                 o_ref):
    tm, d = x_ref.shape
    hd = d // X_HEADS
    heads = [slice(h * hd, (h + 1) * hd) for h in range(X_HEADS)]
    parts = [slice(r, r + tm // POST_ROW_PARTS) for r in range(0, tm, tm // POST_ROW_PARTS)]
    x1, q, s, o = {}, {}, {}, {}
    for i, rows in enumerate(parts):
        x1[i] = (x_ref[rows, :] + _dot(sb_ref[rows, :], wo_ref[0:SB_WIDTH, :])
                 + _dot(ml_ref[rows, :], wo_ref[SB_WIDTH:SB_WIDTH + ML_WIDTH, :]))
    for i in range(len(parts)):
        q[i] = _dot(_rms(x1[i], gx_ref[...]).astype(BF16), wq_ref[...])
    for i in range(len(parts)):
        for h, cols in enumerate(heads):
            qn = _rms(q[i][:, cols], gq_ref[...]).astype(BF16)
            s[i, h] = _dot_nt(qn, k_ref[:, cols]) * (hd ** -0.5)
    for i in range(len(parts)):
        outs = []
        for h, cols in enumerate(heads):
            e = jnp.exp(s[i, h] - jnp.max(s[i, h], axis=-1, keepdims=True))
            part = _dot(e.astype(BF16), v_ref[:, cols]) / jnp.sum(e, axis=-1, keepdims=True)
            outs.append(part.astype(BF16))
        o[i] = jnp.concatenate(outs, axis=1)
    for i, rows in enumerate(parts):
        o_ref[rows, :] = x1[i] + _dot(o[i], wxo_ref[...])


def _post(x, sb, ml, wo, gx, wxq, gq, kmem, vmem, wxo, seq, layer):
    t, d = x.shape
    tm = min(POST_TM, seq)
    per_seq = seq // tm
    m = kmem.shape[0] // (t // seq)
    row = lambda i: (i, 0)
    fixed = lambda i: (0, 0)
    mem_blk = pl.BlockSpec((m, d), lambda i: (i // per_seq, 0))
    return pl.pallas_call(
        _post_kernel,
        grid=(t // tm,),
        in_specs=[
            pl.BlockSpec((tm, d), row),
            pl.BlockSpec((tm, SB_WIDTH), row),
            pl.BlockSpec((tm, ML_WIDTH), row),
            _layer_weight((d, d), layer),
            pl.BlockSpec((1, d), fixed),
            _layer_weight((d, d), layer),
            pl.BlockSpec((1, d // X_HEADS), fixed),
            mem_blk, mem_blk,
            _layer_weight((d, d), layer),
        ],
        out_specs=pl.BlockSpec((tm, d), row),
        out_shape=jax.ShapeDtypeStruct((t, d), F32),
        compiler_params=_params(("parallel",)),
        name="post",
    )(x, sb, ml, wo, gx.reshape(1, d), wxq, gq.reshape(1, d // X_HEADS), kmem, vmem, wxo)


def _layer(x, mem2d, batch, seq, layer, p):
    l = layer
    x = _ffn(x, p["g_ff1"][l], p["w_ff1_gate"], p["w_ff1_up"], p["w_ff1_down"], l)

    s, w = SB_WIDTH, ML_WIDTH
    n_gate = 2 * ML_HEADS
    w_in = p["w_in"][l]
    ml0 = 3 * s
    w_ml_qko = jnp.concatenate([w_in[:, ml0:ml0 + 2 * w], w_in[:, ml0 + 3 * w:ml0 + 4 * w]], axis=1)
    sbq, sbkt, sbv, mlqt, mlk, mlvt, mlo, gates = _proj(
        x, p["g_mix"][l], w_in[:, 0:s], w_in[:, s:2 * s].T, w_in[:, 2 * s:3 * s],
        w_ml_qko, w_in[:, ml0 + 2 * w:ml0 + 3 * w].T, w_in[:, ml0 + 4 * w:].T, p["w_conv"][l],
        p["b_conv"][l].reshape(1, 2 * w), seq)

    sb = _sb_attention(sbq, sbkt, sbv, batch, seq)

    seq3 = lambda a: a.reshape(batch, seq, w)
    ml = _mlstm(mlqt, seq3(mlk), mlvt, seq3(mlo), gates, p["b_gate"][l].reshape(n_gate, 1),
                p["g_mlstm_head"][l].reshape(1, w)).reshape(batch * seq, w)

    kmem, vmem = _memkv(mem2d, p["g_mem"][l], p["w_xk"], p["w_xv"], p["g_knorm"][l], batch, l)
    x = _post(x, sb, ml, p["w_out"], p["g_xattn"][l], p["w_xq"], p["g_qnorm"][l], kmem, vmem,
              p["w_xo"], seq, l)
    x = _ffn(x, p["g_ff2"][l], p["w_ff2_gate"], p["w_ff2_up"], p["w_ff2_down"], l)
    return x


_MATMUL_WEIGHTS = ("w_ff1_gate", "w_ff1_up", "w_ff1_down", "w_in", "w_out", "w_xq", "w_xk", "w_xv",
                   "w_xo", "w_ff2_gate", "w_ff2_up", "w_ff2_down")


def kernel(x, mem, g_ff1, w_ff1_gate, w_ff1_up, w_ff1_down, g_mix, w_in, b_gate, w_conv, b_conv, g_mlstm_head, w_out, g_xattn, g_mem, w_xq, w_xk, w_xv, g_qnorm, g_knorm, w_xo, g_ff2, w_ff2_gate, w_ff2_up, w_ff2_down):
    batch, seq, d = x.shape
    depth = g_ff1.shape[0]
    p = dict(g_ff1=g_ff1, w_ff1_gate=w_ff1_gate, w_ff1_up=w_ff1_up, w_ff1_down=w_ff1_down,
             g_mix=g_mix, w_in=w_in, b_gate=b_gate, w_conv=w_conv, b_conv=b_conv,
             g_mlstm_head=g_mlstm_head, w_out=w_out, g_xattn=g_xattn, g_mem=g_mem, w_xq=w_xq,
             w_xk=w_xk, w_xv=w_xv, g_qnorm=g_qnorm, g_knorm=g_knorm, w_xo=w_xo, g_ff2=g_ff2,
             w_ff2_gate=w_ff2_gate, w_ff2_up=w_ff2_up, w_ff2_down=w_ff2_down)
    for name in _MATMUL_WEIGHTS:
        p[name] = p[name].astype(BF16)
    h = x.reshape(batch * seq, d)
    mem2d = mem.reshape(batch * mem.shape[1], d)
    for l in range(depth):
        h = _layer(h, mem2d, batch, seq, l, p)
    return h.reshape(batch, seq, d)
```

```python
import functools

import jax
import jax.numpy as jnp
from jax import lax
from jax.experimental import pallas as pl
from jax.experimental.pallas import tpu as pltpu

F32 = jnp.float32
BF16 = jnp.bfloat16

EPS = 1e-6
LOG2E = 1.4426950408889634
SB_HEADS = 8
SB_HEAD_DIM = 64
SB_WIDTH = SB_HEADS * SB_HEAD_DIM
ML_HEADS = 4
ML_HEAD_DIM = 128
ML_WIDTH = ML_HEADS * ML_HEAD_DIM
ML_KERNEL_CHUNK = 128
ML_STATE_PAD = 16
CONV_WIDTH = 4
X_HEADS = 4

V7X_VMEM_BYTES = 64 * 1024 * 1024
VMEM_LIMIT_BYTES = V7X_VMEM_BYTES - 8 * 1024 * 1024
LANES = 128
SUBLANES = 8

FFN_TM = 1024
FFN_TF = 256
PROJ_TM = 512
POST_TM = 1024
POST_ROW_PARTS = 4
SB_TQ = 256
SB_TK = 256
SB_PAIRS_PER_STEP = 4
SB_STAGE_SKEW = 1
SB_STICK_GONE_LOG2 = 152.0
SB_MASKED_LOGIT = -1e30
SB_TOP_ROWS = 176
ML_BATCH_PER_STEP = 4
ML_SEQ_BLOCK = 512


def _params(semantics):
    return pltpu.CompilerParams(dimension_semantics=semantics,
                                vmem_limit_bytes=VMEM_LIMIT_BYTES)


def _rms(x, g):
    return x * lax.rsqrt(jnp.mean(x * x, axis=-1, keepdims=True) + EPS) * g


def _dot(a, b):
    return jnp.dot(a, b, preferred_element_type=F32)


def _dot_nt(a, b):
    return lax.dot_general(a, b, (((1,), (1,)), ((), ())), preferred_element_type=F32)


def _log_sigmoid(x):
    return -(jnp.maximum(-x, 0.0) + jnp.log1p(jnp.exp(-jnp.abs(x))))


def _layer_weight(shape, layer):
    return pl.BlockSpec((None,) + shape, lambda *_: (layer, 0, 0), pipeline_mode=pl.Buffered(1))


def _ffn_kernel(x_ref, g_ref, wg_hbm, wu_hbm, wd_hbm, o_ref,
                wg_ref, wu_ref, wd_ref, stage_g, stage_u, stage_d, sems, *, tf, layer):
    n_chunks = wg_ref.shape[1] // tf
    chunk = lambda c: slice(c * tf, (c + 1) * tf)

    @pl.when(pl.program_id(0) == 0)
    def _():
        def copies(c, slot):
            return (pltpu.make_async_copy(wg_hbm.at[layer, :, chunk(c)], stage_g.at[slot],
                                          sems.at[0, slot]),
                    pltpu.make_async_copy(wu_hbm.at[layer, :, chunk(c)], stage_u.at[slot],
                                          sems.at[1, slot]),
                    pltpu.make_async_copy(wd_hbm.at[layer, chunk(c), :], stage_d.at[slot],
                                          sems.at[2, slot]))

        for copy in copies(0, 0):
            copy.start()
        for c in range(n_chunks):
            slot = c % 2
            if c + 1 < n_chunks:
                for copy in copies(c + 1, 1 - slot):
                    copy.start()
            for copy in copies(c, slot):
                copy.wait()
            wg_ref[:, chunk(c)] = stage_g[slot].astype(BF16)
            wu_ref[:, chunk(c)] = stage_u[slot].astype(BF16)
            wd_ref[chunk(c), :] = stage_d[slot].astype(BF16)

    x = x_ref[...]
    xn = _rms(x, g_ref[...]).astype(BF16)
    hidden = []
    for c in range(n_chunks):
        gate = _dot(xn, wg_ref[:, chunk(c)])
        up = _dot(xn, wu_ref[:, chunk(c)])
        hidden.append((gate * jax.nn.sigmoid(gate) * up).astype(BF16))
    o_ref[...] = x + 0.5 * _dot(jnp.concatenate(hidden, axis=1), wd_ref[...])


def _ffn(x, g, wg, wu, wd, layer):
    t, d = x.shape
    f = wg.shape[2]
    tm, tf = min(FFN_TM, t), min(FFN_TF, f)
    hbm = pl.BlockSpec(memory_space=pl.ANY)
    return pl.pallas_call(
        functools.partial(_ffn_kernel, tf=tf, layer=layer),
        grid=(t // tm,),
        in_specs=[
            pl.BlockSpec((tm, d), lambda i: (i, 0)),
            pl.BlockSpec((1, d), lambda i: (0, 0)),
            hbm, hbm, hbm,
        ],
        out_specs=pl.BlockSpec((tm, d), lambda i: (i, 0)),
        out_shape=jax.ShapeDtypeStruct((t, d), F32),
        scratch_shapes=[
            pltpu.VMEM((d, f), BF16), pltpu.VMEM((d, f), BF16), pltpu.VMEM((f, d), BF16),
            pltpu.VMEM((2, d, tf), F32), pltpu.VMEM((2, d, tf), F32), pltpu.VMEM((2, tf, d), F32),
            pltpu.SemaphoreType.DMA((3, 2)),
        ],
        compiler_params=_params(("arbitrary",)),
        name="ffn",
    )(x, g.reshape(1, d), wg, wu, wd)


def _proj_kernel(x_ref, xprev_ref, g_ref, win_ref, wint_ref, wconv_ref, bconv_ref,
                 sbq_ref, sbkt_ref, sbv_ref, mlqt_ref, mlk_ref, mlvt_ref, mlo_ref, gates_ref,
                 *, tiles_per_seq):
    tm = x_ref.shape[0]
    s, w = SB_WIDTH, ML_WIDTH
    ml0 = 3 * s
    wq_ref = win_ref.at[:, 0:s]
    wv_ref = win_ref.at[:, 2 * s:3 * s]
    wml_ref = win_ref.at[:, ml0:ml0 + 2 * w]
    wmlo_ref = win_ref.at[:, ml0 + 3 * w:ml0 + 4 * w]
    wkt_ref = wint_ref.at[0:s, :]
    wmlvt_ref = wint_ref.at[s:s + w, :]
    wgate_ref = wint_ref.at[s + w:s + w + 2 * ML_HEADS, :]
    u = _rms(x_ref[...], g_ref[...]).astype(BF16)
    u_prev = _rms(xprev_ref[...], g_ref[...]).astype(BF16)
    first = pl.program_id(0) % tiles_per_seq == 0

    def conv_silu(c0, c1):
        cur = _dot(u, wml_ref[:, c0:c1])
        prev = jnp.where(first, 0.0, _dot(u_prev, wml_ref[:, c0:c1]))
        ext = jnp.concatenate([prev, cur], axis=0)
        y = bconv_ref[:, c0:c1]
        for j in range(CONV_WIDTH):
            back = CONV_WIDTH - 1 - j
            window = pltpu.roll(ext, back, axis=0) if back else ext
            y = y + window[SUBLANES:, :] * wconv_ref[j:j + 1, c0:c1]
        return y * jax.nn.sigmoid(y)

    half = w // 2
    q_scale = ML_HEAD_DIM ** -0.5
    mlqt_ref[0:half, :] = (conv_silu(0, half) * q_scale).T.astype(BF16)
    sbq_ref[...] = (_dot(u, wq_ref[...]) * (LOG2E * SB_HEAD_DIM ** -0.5)).astype(BF16)
    mlqt_ref[half:w, :] = (conv_silu(half, w) * q_scale).T.astype(BF16)
    sbkt_ref[...] = _dot_nt(wkt_ref[...], u).astype(BF16)
    mlk_ref[:, 0:half] = conv_silu(w, w + half).astype(BF16)
    sbv_ref[...] = _dot(u, wv_ref[...]).astype(BF16)
    mlk_ref[:, half:w] = conv_silu(w + half, 2 * w).astype(BF16)
    mlvt_ref[...] = _dot_nt(wmlvt_ref[...], u).astype(BF16)
    mlo_ref[...] = _dot(u, wmlo_ref[...])
    gates_ref[...] = _dot_nt(wgate_ref[...], u)


def _proj(x, g, w_in, w_in_t, wconv, bconv, seq, layer):
    t, d = x.shape
    tm = min(PROJ_TM, seq)
    tiles_per_seq = seq // tm
    row = lambda i: (i, 0)
    fixed = lambda i: (0, 0)
    whole = lambda a: pl.BlockSpec(a.shape, fixed, pipeline_mode=pl.Buffered(1))
    seq_t = lambda rows: pl.BlockSpec((None, rows, tm),
                                      lambda i: (i // tiles_per_seq, 0, i % tiles_per_seq))
    return pl.pallas_call(
        functools.partial(_proj_kernel, tiles_per_seq=tiles_per_seq),
        grid=(t // tm,),
        in_specs=[
            pl.BlockSpec((tm, d), row),
            pl.BlockSpec((SUBLANES, d), lambda i: (jnp.maximum(i * (tm // SUBLANES) - 1, 0), 0)),
            pl.BlockSpec((1, d), fixed),
            _layer_weight(w_in.shape[1:], layer), _layer_weight(w_in_t.shape[1:], layer),
            whole(wconv), whole(bconv),
        ],
        out_specs=[
            pl.BlockSpec((tm, SB_WIDTH), row),
            pl.BlockSpec((SB_WIDTH, tm), lambda i: (0, i)),
            pl.BlockSpec((tm, SB_WIDTH), row),
            seq_t(ML_WIDTH),
            pl.BlockSpec((tm, ML_WIDTH), row),
            seq_t(ML_WIDTH),
            pl.BlockSpec((tm, ML_WIDTH), row),
            seq_t(2 * ML_HEADS),
        ],
        out_shape=[
            jax.ShapeDtypeStruct((t, SB_WIDTH), BF16),
            jax.ShapeDtypeStruct((SB_WIDTH, t), BF16),
            jax.ShapeDtypeStruct((t, SB_WIDTH), BF16),
            jax.ShapeDtypeStruct((t // seq, ML_WIDTH, seq), BF16),
            jax.ShapeDtypeStruct((t, ML_WIDTH), BF16),
            jax.ShapeDtypeStruct((t // seq, ML_WIDTH, seq), BF16),
            jax.ShapeDtypeStruct((t, ML_WIDTH), F32),
            jax.ShapeDtypeStruct((t // seq, 2 * ML_HEADS, seq), F32),
        ],
        compiler_params=_params(("parallel",)),
        name="proj",
    )(x, x, g.reshape(1, d), w_in, w_in_t, wconv, bconv)


def _sb_kernel(q_ref, kt_ref, v_ref, o_ref, acc_ref, *, tq, tk, pairs):
    i = pl.program_id(2)
    half = SB_HEAD_DIM
    n_heads = 2 * pairs
    top = min(SB_TOP_ROWS, tq)
    lane = lax.broadcasted_iota(jnp.int32, (tq, LANES), 1)
    q_heads = []
    for p in range(pairs):
        q = q_ref[:, p * LANES:(p + 1) * LANES]
        zero = jnp.zeros_like(q)
        q_heads += [jnp.where(lane < half, q, zero), jnp.where(lane >= half, q, zero)]
    pair = lambda h: slice((h // 2) * LANES, (h // 2 + 1) * LANES)

    def suffix_ones(n):
        r = lax.broadcasted_iota(jnp.int32, (2 * n, n), 0)
        c = lax.broadcasted_iota(jnp.int32, (2 * n, n), 1)
        return jnp.where((r & (n - 1)) > c, 1.0, 0.0).astype(BF16)

    ones = {tk: suffix_ones(tk)}

    acc_ref[...] = jnp.zeros_like(acc_ref)

    def block(kb, carries, jobs, masked):
        start = pl.multiple_of(kb * tk, tk)
        chains = [(job, h) for job in jobs for h in range(n_heads)]
        z2, nl, later, fresh = {}, {}, {}, {}

        def strict(job):
            r0, r1, nk = job
            tr = lax.broadcasted_iota(jnp.int32, (r1 - r0, nk), 0) + r0
            tc = lax.broadcasted_iota(jnp.int32, (r1 - r0, nk), 1)
            return tc < tr

        def scores(ch):
            (r0, r1, nk), h = ch
            z = _dot(q_heads[h][r0:r1], kt_ref[pair(h), pl.ds(start, nk)])
            if masked:
                z = jnp.where(strict(ch[0]), z, SB_MASKED_LOGIT)
            z2[ch] = z

        def suffix_sums(ch):
            (r0, r1, nk), h = ch
            z = z2[ch]
            x = jnp.maximum(z, 0.0) + jnp.log(1.0 + jnp.exp2(-jnp.abs(z))) * LOG2E
            hi = x.astype(BF16)
            lo = (x - hi.astype(F32)).astype(BF16)
            nl[ch] = x
            later[ch] = _dot(jnp.concatenate([hi, lo], axis=1), ones[nk])

        def weights_times_values(ch):
            (r0, r1, nk), h = ch
            carry = carries[h][r0:r1]
            a = jnp.exp2(z2.pop(ch) - (nl[ch] + later[ch] + carry))
            acc_ref[h, r0:r1, :] += _dot(a.astype(BF16), v_ref[pl.ds(start, nk), pair(h)])
            fresh[ch] = carry + later.pop(ch)[:, 0:1] + nl.pop(ch)[:, 0:1]

        n = len(chains)
        for step in range(n + 2 * SB_STAGE_SKEW):
            if step < n:
                scores(chains[step])
            if 0 <= step - SB_STAGE_SKEW < n:
                suffix_sums(chains[step - SB_STAGE_SKEW])
            if 0 <= step - 2 * SB_STAGE_SKEW < n:
                weights_times_values(chains[step - 2 * SB_STAGE_SKEW])

        out = []
        for h in range(n_heads):
            pieces, row = [], 0
            for job in sorted(jobs):
                r0, r1, _ = job
                if r0 > row:
                    pieces.append(carries[h][row:r0])
                pieces.append(fresh[(job, h)])
                row = r1
            if row < tq:
                pieces.append(carries[h][row:tq])
            out.append(pieces[0] if len(pieces) == 1 else jnp.concatenate(pieces, axis=0))
        return tuple(out)

    zeros = jnp.zeros((tq, 1), F32)
    carries = block(i, (zeros,) * n_heads, [(0, tq, tk)], True)

    def least(carries):
        rows = functools.reduce(jnp.minimum, carries)
        below = jnp.min(rows[top:]) if top < tq else jnp.float32(SB_STICK_GONE_LOG2)
        return jnp.min(rows[:top]), below

    def walk(state, r1, watch):
        def live(state):
            return jnp.logical_and(state[0] < i, state[watch] < SB_STICK_GONE_LOG2)

        def body(state):
            n, _, _, carries = state
            carries = block(i - 1 - n, carries, [(0, r1, tk)], False)
            return (n + 1,) + least(carries) + (carries,)

        return lax.while_loop(live, body, state)

    state = (jnp.int32(0),) + least(carries) + (carries,)
    state = walk(state, tq, 2)
    walk(state, top, 1)
    for p in range(pairs):
        o_ref[:, p * LANES:(p + 1) * LANES] = jnp.where(
            lane < half, acc_ref[2 * p], acc_ref[2 * p + 1]).astype(o_ref.dtype)


def _sb_attention(sbq, sbkt, sbv, batch, seq):
    t = sbq.shape[0]
    tq, tk = min(SB_TQ, seq), min(SB_TK, seq)
    assert tq == tk
    nq = seq // tq
    pairs = SB_PAIRS_PER_STEP
    width = pairs * LANES
    groups = SB_WIDTH // width
    return pl.pallas_call(
        functools.partial(_sb_kernel, tq=tq, tk=tk, pairs=pairs),
        grid=(batch, groups, nq),
        in_specs=[
            pl.BlockSpec((tq, width), lambda b, p, i: (b * nq + i, p)),
            pl.BlockSpec((width, seq), lambda b, p, i: (p, b)),
            pl.BlockSpec((seq, width), lambda b, p, i: (b, p)),
        ],
        out_specs=pl.BlockSpec((tq, width), lambda b, p, i: (b * nq + i, p)),
        out_shape=jax.ShapeDtypeStruct((t, SB_WIDTH), BF16),
        scratch_shapes=[pltpu.VMEM((2 * pairs, tq, LANES), F32)],
        compiler_params=_params(("parallel", "parallel", "arbitrary")),
        name="sb_attention",
    )(sbq, sbkt, sbv)


def _mlstm_kernel(qt_ref, k_ref, vt_ref, og_ref, grow_ref, brow_ref,
                  ghead_ref, o_ref, c_ref, m_ref, gate_ref, scan_ref, *, n_batch, n_chunks):
    L, d, H = ML_KERNEL_CHUNK, ML_HEAD_DIM, ML_HEADS
    chains = [(b, h) for b in range(n_batch) for h in range(H)]

    @pl.when(pl.program_id(1) == 0)
    def _():
        c_ref[...] = jnp.zeros_like(c_ref)
        m_ref[...] = jnp.zeros_like(m_ref)

    ss = lax.broadcasted_iota(jnp.int32, (L, L), 0)
    tt = lax.broadcasted_iota(jnp.int32, (L, L), 1)
    causal = ss <= tt
    eye = ss == tt
    gate_lane = lax.broadcasted_iota(jnp.int32, (2 * H, L), 1)
    gain_col = [jnp.sum(jnp.where(eye, ghead_ref[:, h * d:(h + 1) * d], 0.0), axis=1, keepdims=True)
                for h in range(H)]

    for b in range(n_batch):
        for c in range(n_chunks):
            lanes = slice(c * L, (c + 1) * L)
            g = grow_ref[b, :, lanes] + brow_ref[...]
            scan = _log_sigmoid(g)
            for k in range(L.bit_length() - 1):
                shifted = pltpu.roll(scan, 1 << k, axis=1)
                scan = scan + jnp.where(gate_lane >= (1 << k), shifted, 0.0)
            gate_ref[b, :, lanes] = g
            scan_ref[b, :, lanes] = scan

    def chunk(ci, ms):
        rows = pl.ds(pl.multiple_of(ci * L, L), L)
        lf_scan, g_row = {}, {}
        for b in range(n_batch):
            g_row[b] = gate_ref[b, :, rows]
            lf_scan[b] = scan_ref[b, :, rows]

        kb, qt, vt, c_st, s_raw, q_c = {}, {}, {}, {}, {}, {}
        for ch in chains:
            b, h = ch
            cols = slice(h * d, (h + 1) * d)
            kb[ch] = k_ref[b, rows, cols]
            qt[ch] = qt_ref[b, cols, rows]
            vt[ch] = vt_ref[b, cols, rows]
            c_st[ch] = c_ref[b * H + h]
            s_raw[ch] = _dot(kb[ch], qt[ch])
            q_c[ch] = _dot(c_st[ch].astype(BF16), qt[ch])

        li_row, bcum, g_col = {}, {}, {}
        for ch in chains:
            b, h = ch
            li_row[ch] = g_row[b][h:h + 1, :]
            bcum[ch] = lf_scan[b][H + h:H + h + 1, :]
            g_col[ch] = jnp.sum(jnp.where(eye, li_row[ch] - bcum[ch], 0.0), axis=1, keepdims=True)

        w_inter, m_t, intra, den, w_state, decay, new_ms = {}, {}, {}, {}, {}, {}, []
        for idx, ch in enumerate(chains):
            m_prev = ms[idx]
            dmat = jnp.where(causal, bcum[ch] + g_col.pop(ch), -jnp.inf)
            inter = bcum[ch] + m_prev
            m_t[ch] = jnp.maximum(jnp.max(dmat, axis=0, keepdims=True), inter)
            sc = s_raw.pop(ch) * jnp.exp(dmat - m_t[ch])
            w_inter[ch] = jnp.exp(inter - m_t[ch])
            den[ch] = jnp.sum(sc, axis=0, keepdims=True)
            intra[ch] = _dot(vt[ch], sc.astype(BF16))
            btot = bcum[ch][:, L - 1:L]
            m_new = m_t[ch][:, L - 1:L]
            w_state[ch] = jnp.exp(btot - bcum.pop(ch) + li_row.pop(ch) - m_new)
            decay[ch] = jnp.exp(btot + m_prev - m_new)
            new_ms.append(m_new)

        for ch in chains:
            b, h = ch
            w = w_state.pop(ch)
            lhs = jnp.concatenate([vt.pop(ch).astype(F32) * w,
                                   jnp.broadcast_to(w, (ML_STATE_PAD, L))], axis=0)
            c_ref[b * H + h] = decay.pop(ch) * c_st.pop(ch) + _dot(lhs.astype(BF16), kb.pop(ch))

        for ch in chains:
            b, h = ch
            cols = slice(h * d, (h + 1) * d)
            readout = q_c.pop(ch)
            num = intra.pop(ch) + w_inter[ch] * readout[0:d]
            total = den.pop(ch) + w_inter.pop(ch) * readout[d:d + 1]
            hval = num / jnp.maximum(jnp.abs(total), jnp.exp(-m_t.pop(ch)))
            ssq = jnp.mean(hval * hval, axis=0, keepdims=True)
            hn = (hval * lax.rsqrt(ssq + EPS) * gain_col[h]).T
            og = og_ref[b, rows, cols]
            o_ref[b, rows, cols] = (hn * jax.nn.sigmoid(og)).astype(o_ref.dtype)
        return tuple(new_ms)

    ms = tuple(m_ref[i] for i in range(len(chains)))
    ms = lax.fori_loop(0, n_chunks, chunk, ms)
    for i, m in enumerate(ms):
        m_ref[i] = m


def _mlstm(mlqt, mlk, mlvt, mlo, grow, brow, ghead):
    batch, seq, width = mlk.shape
    nb = min(ML_BATCH_PER_STEP, batch)
    sblk = min(ML_SEQ_BLOCK, seq)
    chunks = sblk // ML_KERNEL_CHUNK
    n_gate = 2 * ML_HEADS
    seq_blk = pl.BlockSpec((nb, sblk, width), lambda b, s: (b, s, 0))
    seq_blk_t = pl.BlockSpec((nb, width, sblk), lambda b, s: (b, 0, s))
    fixed2 = lambda b, s: (0, 0)
    return pl.pallas_call(
        functools.partial(_mlstm_kernel, n_batch=nb, n_chunks=chunks),
        grid=(batch // nb, seq // sblk),
        in_specs=[
            seq_blk_t, seq_blk, seq_blk_t, seq_blk,
            pl.BlockSpec((nb, n_gate, sblk), lambda b, s: (b, 0, s)),
            pl.BlockSpec(brow.shape, fixed2),
            pl.BlockSpec(ghead.shape, fixed2),
        ],
        out_specs=seq_blk,
        out_shape=jax.ShapeDtypeStruct((batch, seq, width), BF16),
        scratch_shapes=[pltpu.VMEM((nb * ML_HEADS, ML_HEAD_DIM + ML_STATE_PAD, ML_HEAD_DIM), F32),
                        pltpu.VMEM((nb * ML_HEADS, 1, 1), F32),
                        pltpu.VMEM((nb, n_gate, sblk), F32),
                        pltpu.VMEM((nb, n_gate, sblk), F32)],
        compiler_params=_params(("parallel", "arbitrary")),
        name="mlstm",
    )(mlqt, mlk, mlvt, mlo, grow, brow, ghead)


def _memkv_kernel(mem_ref, gm_ref, wk_ref, wv_ref, gk_ref, k_ref, v_ref):
    mn = _rms(mem_ref[...], gm_ref[...]).astype(BF16)
    k = _dot(mn, wk_ref[...])
    hd = k.shape[1] // X_HEADS
    for h in range(X_HEADS):
        k_ref[:, h * hd:(h + 1) * hd] = _rms(k[:, h * hd:(h + 1) * hd], gk_ref[...]).astype(BF16)
    v_ref[...] = _dot(mn, wv_ref[...]).astype(BF16)


def _memkv(mem2d, gm, wk, wv, gk, batch, layer):
    rows, d = mem2d.shape
    m = rows // batch
    blk = pl.BlockSpec((m, d), lambda b: (b, 0))
    fixed = lambda b: (0, 0)
    return pl.pallas_call(
        _memkv_kernel,
        grid=(batch,),
        in_specs=[blk, pl.BlockSpec((1, d), fixed), _layer_weight((d, d), layer),
                  _layer_weight((d, d), layer), pl.BlockSpec((1, d // X_HEADS), fixed)],
        out_specs=[blk, blk],
        out_shape=[jax.ShapeDtypeStruct((rows, d), BF16)] * 2,
        compiler_params=_params(("parallel",)),
        name="memkv",
    )(mem2d, gm.reshape(1, d), wk, wv, gk.reshape(1, d // X_HEADS))


def _post_kernel(x_ref, sb_ref, ml_ref, wo_ref, gx_ref, wq_ref, gq_ref, k_ref, v_ref, wxo_ref,
                 o_ref):
    tm, d = x_ref.shape
    hd = d // X_HEADS
    heads = [slice(h * hd, (h + 1) * hd) for h in range(X_HEADS)]
    parts = [slice(r, r + tm // POST_ROW_PARTS) for r in range(0, tm, tm // POST_ROW_PARTS)]
    x1, q, s, o = {}, {}, {}, {}
    for i, rows in enumerate(parts):
        x1[i] = (x_ref[rows, :] + _dot(sb_ref[rows, :], wo_ref[0:SB_WIDTH, :])
                 + _dot(ml_ref[rows, :], wo_ref[SB_WIDTH:SB_WIDTH + ML_WIDTH, :]))
    for i in range(len(parts)):
        q[i] = _dot(_rms(x1[i], gx_ref[...]).astype(BF16), wq_ref[...])
    for i in range(len(parts)):
        for h, cols in enumerate(heads):
            qn = _rms(q[i][:, cols], gq_ref[...]).astype(BF16)
            s[i, h] = _dot_nt(qn, k_ref[:, cols]) * (hd ** -0.5)
    for i in range(len(parts)):
        outs = []
        for h, cols in enumerate(heads):
            e = jnp.exp(s[i, h] - jnp.max(s[i, h], axis=-1, keepdims=True))
            part = _dot(e.astype(BF16), v_ref[:, cols]) / jnp.sum(e, axis=-1, keepdims=True)
            outs.append(part.astype(BF16))
        o[i] = jnp.concatenate(outs, axis=1)
    for i, rows in enumerate(parts):
        o_ref[rows, :] = x1[i] + _dot(o[i], wxo_ref[...])


def _post(x, sb, ml, wo, gx, wxq, gq, kmem, vmem, wxo, seq, layer):
    t, d = x.shape
    tm = min(POST_TM, seq)
    per_seq = seq // tm
    m = kmem.shape[0] // (t // seq)
    row = lambda i: (i, 0)
    fixed = lambda i: (0, 0)
    mem_blk = pl.BlockSpec((m, d), lambda i: (i // per_seq, 0))
    return pl.pallas_call(
        _post_kernel,
        grid=(t // tm,),
        in_specs=[
            pl.BlockSpec((tm, d), row),
            pl.BlockSpec((tm, SB_WIDTH), row),
            pl.BlockSpec((tm, ML_WIDTH), row),
            _layer_weight((d, d), layer),
            pl.BlockSpec((1, d), fixed),
            _layer_weight((d, d), layer),
            pl.BlockSpec((1, d // X_HEADS), fixed),
            mem_blk, mem_blk,
            _layer_weight((d, d), layer),
        ],
        out_specs=pl.BlockSpec((tm, d), row),
        out_shape=jax.ShapeDtypeStruct((t, d), F32),
        compiler_params=_params(("parallel",)),
        name="post",
    )(x, sb, ml, wo, gx.reshape(1, d), wxq, gq.reshape(1, d // X_HEADS), kmem, vmem, wxo)


def _layer(x, mem2d, batch, seq, layer, p):
    l = layer
    x = _ffn(x, p["g_ff1"][l], p["w_ff1_gate"], p["w_ff1_up"], p["w_ff1_down"], l)

    s, w = SB_WIDTH, ML_WIDTH
    n_gate = 2 * ML_HEADS
    sbq, sbkt, sbv, mlqt, mlk, mlvt, mlo, gates = _proj(
        x, p["g_mix"][l], p["w_in"], p["w_in_t"], p["w_conv"][l],
        p["b_conv"][l].reshape(1, 2 * w), seq, l)

    sb = _sb_attention(sbq, sbkt, sbv, batch, seq)

    seq3 = lambda a: a.reshape(batch, seq, w)
    ml = _mlstm(mlqt, seq3(mlk), mlvt, seq3(mlo), gates, p["b_gate"][l].reshape(n_gate, 1),
                p["g_mlstm_head"][l].reshape(1, w)).reshape(batch * seq, w)

    kmem, vmem = _memkv(mem2d, p["g_mem"][l], p["w_xk"], p["w_xv"], p["g_knorm"][l], batch, l)
    x = _post(x, sb, ml, p["w_out"], p["g_xattn"][l], p["w_xq"], p["g_qnorm"][l], kmem, vmem,
              p["w_xo"], seq, l)
    x = _ffn(x, p["g_ff2"][l], p["w_ff2_gate"], p["w_ff2_up"], p["w_ff2_down"], l)
    return x


def _transposed_in_columns(w_in):
    s, w = SB_WIDTH, ML_WIDTH
    ml0 = 3 * s
    cols = [w_in[:, :, s:2 * s], w_in[:, :, ml0 + 2 * w:ml0 + 3 * w], w_in[:, :, ml0 + 4 * w:]]
    return jnp.concatenate(cols, axis=2).transpose(0, 2, 1)


_MATMUL_WEIGHTS = ("w_in", "w_out", "w_xq", "w_xk", "w_xv", "w_xo")


def kernel(x, mem, g_ff1, w_ff1_gate, w_ff1_up, w_ff1_down, g_mix, w_in, b_gate, w_conv, b_conv, g_mlstm_head, w_out, g_xattn, g_mem, w_xq, w_xk, w_xv, g_qnorm, g_knorm, w_xo, g_ff2, w_ff2_gate, w_ff2_up, w_ff2_down):
    batch, seq, d = x.shape
    depth = g_ff1.shape[0]
    p = dict(g_ff1=g_ff1, w_ff1_gate=w_ff1_gate, w_ff1_up=w_ff1_up, w_ff1_down=w_ff1_down,
             g_mix=g_mix, w_in=w_in, b_gate=b_gate, w_conv=w_conv, b_conv=b_conv,
             g_mlstm_head=g_mlstm_head, w_out=w_out, g_xattn=g_xattn, g_mem=g_mem, w_xq=w_xq,
             w_xk=w_xk, w_xv=w_xv, g_qnorm=g_qnorm, g_knorm=g_knorm, w_xo=w_xo, g_ff2=g_ff2,
             w_ff2_gate=w_ff2_gate, w_ff2_up=w_ff2_up, w_ff2_down=w_ff2_down)
    for name in _MATMUL_WEIGHTS:
        p[name] = p[name].astype(BF16)
    p["w_in_t"] = _transposed_in_columns(p["w_in"])
    h = x.reshape(batch * seq, d)
    mem2d = mem.reshape(batch * mem.shape[1], d)
    for l in range(depth):
        h = _layer(h, mem2d, batch, seq, l, p)
    return h.reshape(batch, seq, d)
```

```python
import functools

import jax
import jax.numpy as jnp
from jax import lax
from jax.experimental import pallas as pl
from jax.experimental.pallas import tpu as pltpu

F32 = jnp.float32
BF16 = jnp.bfloat16

EPS = 1e-6
LOG2E = 1.4426950408889634
SB_HEADS = 8
SB_HEAD_DIM = 64
SB_WIDTH = SB_HEADS * SB_HEAD_DIM
ML_HEADS = 4
ML_HEAD_DIM = 128
ML_WIDTH = ML_HEADS * ML_HEAD_DIM
ML_KERNEL_CHUNK = 128
ML_STATE_PAD = 16
CONV_WIDTH = 4
X_HEADS = 4

V7X_VMEM_BYTES = 64 * 1024 * 1024
VMEM_LIMIT_BYTES = V7X_VMEM_BYTES - 8 * 1024 * 1024
LANES = 128
SUBLANES = 8

FFN_TM = 1024
FFN_TF = 256
PROJ_TM = 1024
POST_TM = 1024
POST_ROW_PARTS = 4
SB_TQ = 256
SB_TK = 256
SB_PAIRS_PER_STEP = 4
SB_STAGE_SKEW = 1
SB_STICK_GONE_LOG2 = 152.0
SB_MASKED_LOGIT = -1e30
SB_TOP_ROWS = 176
ML_BATCH_PER_STEP = 4
ML_SEQ_BLOCK = 512


def _params(semantics):
    return pltpu.CompilerParams(dimension_semantics=semantics,
                                vmem_limit_bytes=VMEM_LIMIT_BYTES)


def _rms(x, g):
    return x * lax.rsqrt(jnp.mean(x * x, axis=-1, keepdims=True) + EPS) * g


def _dot(a, b):
    return jnp.dot(a, b, preferred_element_type=F32)


def _dot_nt(a, b):
    return lax.dot_general(a, b, (((1,), (1,)), ((), ())), preferred_element_type=F32)


def _log_sigmoid(x):
    return -(jnp.maximum(-x, 0.0) + jnp.log1p(jnp.exp(-jnp.abs(x))))


def _layer_weight(shape, layer):
    return pl.BlockSpec((None,) + shape, lambda *_: (layer, 0, 0), pipeline_mode=pl.Buffered(1))


def _ffn_kernel(x_ref, g_ref, wg_hbm, wu_hbm, wd_hbm, o_ref,
                wg_ref, wu_ref, wd_ref, stage_g, stage_u, stage_d, sems, *, tf, layer):
    n_chunks = wg_ref.shape[1] // tf
    chunk = lambda c: slice(c * tf, (c + 1) * tf)

    @pl.when(pl.program_id(0) == 0)
    def _():
        def copies(c, slot):
            return (pltpu.make_async_copy(wg_hbm.at[layer, :, chunk(c)], stage_g.at[slot],
                                          sems.at[0, slot]),
                    pltpu.make_async_copy(wu_hbm.at[layer, :, chunk(c)], stage_u.at[slot],
                                          sems.at[1, slot]),
                    pltpu.make_async_copy(wd_hbm.at[layer, chunk(c), :], stage_d.at[slot],
                                          sems.at[2, slot]))

        for copy in copies(0, 0):
            copy.start()
        for c in range(n_chunks):
            slot = c % 2
            if c + 1 < n_chunks:
                for copy in copies(c + 1, 1 - slot):
                    copy.start()
            for copy in copies(c, slot):
                copy.wait()
            wg_ref[:, chunk(c)] = stage_g[slot].astype(BF16)
            wu_ref[:, chunk(c)] = stage_u[slot].astype(BF16)
            wd_ref[chunk(c), :] = stage_d[slot].astype(BF16)

    x = x_ref[...]
    xn = _rms(x, g_ref[...]).astype(BF16)
    hidden = []
    for c in range(n_chunks):
        gate = _dot(xn, wg_ref[:, chunk(c)])
        up = _dot(xn, wu_ref[:, chunk(c)])
        hidden.append((gate * jax.nn.sigmoid(gate) * up).astype(BF16))
    o_ref[...] = x + 0.5 * _dot(jnp.concatenate(hidden, axis=1), wd_ref[...])


def _ffn(x, g, wg, wu, wd, layer):
    t, d = x.shape
    f = wg.shape[2]
    tm, tf = min(FFN_TM, t), min(FFN_TF, f)
    assert t % tm == 0 and f % tf == 0
    hbm = pl.BlockSpec(memory_space=pl.ANY)
    return pl.pallas_call(
        functools.partial(_ffn_kernel, tf=tf, layer=layer),
        grid=(t // tm,),
        in_specs=[
            pl.BlockSpec((tm, d), lambda i: (i, 0)),
            pl.BlockSpec((1, d), lambda i: (0, 0)),
            hbm, hbm, hbm,
        ],
        out_specs=pl.BlockSpec((tm, d), lambda i: (i, 0)),
        out_shape=jax.ShapeDtypeStruct((t, d), F32),
        scratch_shapes=[
            pltpu.VMEM((d, f), BF16), pltpu.VMEM((d, f), BF16), pltpu.VMEM((f, d), BF16),
            pltpu.VMEM((2, d, tf), F32), pltpu.VMEM((2, d, tf), F32), pltpu.VMEM((2, tf, d), F32),
            pltpu.SemaphoreType.DMA((3, 2)),
        ],
        compiler_params=_params(("arbitrary",)),
        name="ffn",
    )(x, g.reshape(1, d), wg, wu, wd)


def _proj_kernel(x_ref, xprev_ref, g_ref, win_ref, wgate_ref, wconv_ref, bconv_ref,
                 sbq_ref, sbkt_ref, sbv_ref, mlqt_ref, mlk_ref, mlvt_ref, mlo_ref, gates_ref,
                 wkt_ref, wmlvt_ref, *, tiles_per_seq):
    tm = x_ref.shape[0]
    s, w = SB_WIDTH, ML_WIDTH
    ml0 = 3 * s
    wq_ref = win_ref.at[:, 0:s]
    wv_ref = win_ref.at[:, 2 * s:3 * s]
    wml_ref = win_ref.at[:, ml0:ml0 + 2 * w]
    wmlo_ref = win_ref.at[:, ml0 + 3 * w:ml0 + 4 * w]

    @pl.when(pl.program_id(0) == 0)
    def _():
        wkt_ref[...] = win_ref[:, s:2 * s].astype(F32).T.astype(BF16)
        wmlvt_ref[...] = win_ref[:, ml0 + 2 * w:ml0 + 3 * w].astype(F32).T.astype(BF16)

    u = _rms(x_ref[...], g_ref[...]).astype(BF16)
    u_prev = _rms(xprev_ref[...], g_ref[...]).astype(BF16)
    first = pl.program_id(0) % tiles_per_seq == 0

    def conv_silu(c0, c1):
        cur = _dot(u, wml_ref[:, c0:c1])
        prev = jnp.where(first, 0.0, _dot(u_prev, wml_ref[:, c0:c1]))
        ext = jnp.concatenate([prev, cur], axis=0)
        y = bconv_ref[:, c0:c1]
        for j in range(CONV_WIDTH):
            back = CONV_WIDTH - 1 - j
            window = pltpu.roll(ext, back, axis=0) if back else ext
            y = y + window[SUBLANES:, :] * wconv_ref[j:j + 1, c0:c1]
        return y * jax.nn.sigmoid(y)

    half = w // 2
    q_scale = ML_HEAD_DIM ** -0.5
    mlqt_ref[0:half, :] = (conv_silu(0, half) * q_scale).T.astype(BF16)
    sbq_ref[...] = (_dot(u, wq_ref[...]) * (LOG2E * SB_HEAD_DIM ** -0.5)).astype(BF16)
    mlqt_ref[half:w, :] = (conv_silu(half, w) * q_scale).T.astype(BF16)
    sbkt_ref[...] = _dot_nt(wkt_ref[...], u).astype(BF16)
    mlk_ref[:, 0:half] = conv_silu(w, w + half).astype(BF16)
    sbv_ref[...] = _dot(u, wv_ref[...]).astype(BF16)
    mlk_ref[:, half:w] = conv_silu(w + half, 2 * w).astype(BF16)
    mlvt_ref[...] = _dot_nt(wmlvt_ref[...], u).astype(BF16)
    mlo_ref[...] = _dot(u, wmlo_ref[...])
    gates_ref[...] = _dot_nt(wgate_ref[...], u)


def _proj(x, g, w_in, w_gate_t, wconv, bconv, seq, layer):
    t, d = x.shape
    tm = min(PROJ_TM, seq)
    assert seq % tm == 0 and tm % SUBLANES == 0
    tiles_per_seq = seq // tm
    row = lambda i: (i, 0)
    fixed = lambda i: (0, 0)
    whole = lambda a: pl.BlockSpec(a.shape, fixed, pipeline_mode=pl.Buffered(1))
    seq_t = lambda rows: pl.BlockSpec((None, rows, tm),
                                      lambda i: (i // tiles_per_seq, 0, i % tiles_per_seq))
    return pl.pallas_call(
        functools.partial(_proj_kernel, tiles_per_seq=tiles_per_seq),
        grid=(t // tm,),
        in_specs=[
            pl.BlockSpec((tm, d), row),
            pl.BlockSpec((SUBLANES, d), lambda i: (jnp.maximum(i * (tm // SUBLANES) - 1, 0), 0)),
            pl.BlockSpec((1, d), fixed),
            _layer_weight(w_in.shape[1:], layer), _layer_weight(w_gate_t.shape[1:], layer),
            whole(wconv), whole(bconv),
        ],
        out_specs=[
            pl.BlockSpec((tm, SB_WIDTH), row),
            pl.BlockSpec((SB_WIDTH, tm), lambda i: (0, i)),
            pl.BlockSpec((tm, SB_WIDTH), row),
            seq_t(ML_WIDTH),
            pl.BlockSpec((tm, ML_WIDTH), row),
            seq_t(ML_WIDTH),
            pl.BlockSpec((tm, ML_WIDTH), row),
            seq_t(2 * ML_HEADS),
        ],
        out_shape=[
            jax.ShapeDtypeStruct((t, SB_WIDTH), BF16),
            jax.ShapeDtypeStruct((SB_WIDTH, t), BF16),
            jax.ShapeDtypeStruct((t, SB_WIDTH), BF16),
            jax.ShapeDtypeStruct((t // seq, ML_WIDTH, seq), BF16),
            jax.ShapeDtypeStruct((t, ML_WIDTH), BF16),
            jax.ShapeDtypeStruct((t // seq, ML_WIDTH, seq), BF16),
            jax.ShapeDtypeStruct((t, ML_WIDTH), F32),
            jax.ShapeDtypeStruct((t // seq, 2 * ML_HEADS, seq), F32),
        ],
        scratch_shapes=[pltpu.VMEM((SB_WIDTH, d), BF16), pltpu.VMEM((ML_WIDTH, d), BF16)],
        compiler_params=_params(("arbitrary",)),
        name="proj",
    )(x, x, g.reshape(1, d), w_in, w_gate_t, wconv, bconv)


def _sb_kernel(q_ref, kt_ref, v_ref, o_ref, acc_ref, *, tq, tk, pairs):
    i = pl.program_id(2)
    half = SB_HEAD_DIM
    n_heads = 2 * pairs
    top = min(SB_TOP_ROWS, tq)
    lane = lax.broadcasted_iota(jnp.int32, (tq, LANES), 1)
    q_heads = []
    for p in range(pairs):
        q = q_ref[:, p * LANES:(p + 1) * LANES]
        zero = jnp.zeros_like(q)
        q_heads += [jnp.where(lane < half, q, zero), jnp.where(lane >= half, q, zero)]
    pair = lambda h: slice((h // 2) * LANES, (h // 2 + 1) * LANES)

    def suffix_ones(n):
        r = lax.broadcasted_iota(jnp.int32, (2 * n, n), 0)
        c = lax.broadcasted_iota(jnp.int32, (2 * n, n), 1)
        return jnp.where((r & (n - 1)) > c, 1.0, 0.0).astype(BF16)

    ones = {tk: suffix_ones(tk)}

    acc_ref[...] = jnp.zeros_like(acc_ref)

    def block(kb, carries, jobs, masked):
        start = pl.multiple_of(kb * tk, tk)
        chains = [(job, h) for job in jobs for h in range(n_heads)]
        z2, nl, later, fresh = {}, {}, {}, {}

        def strict(job):
            r0, r1, nk = job
            tr = lax.broadcasted_iota(jnp.int32, (r1 - r0, nk), 0) + r0
            tc = lax.broadcasted_iota(jnp.int32, (r1 - r0, nk), 1)
            return tc < tr

        def scores(ch):
            (r0, r1, nk), h = ch
            z = _dot(q_heads[h][r0:r1], kt_ref[pair(h), pl.ds(start, nk)])
            if masked:
                z = jnp.where(strict(ch[0]), z, SB_MASKED_LOGIT)
            z2[ch] = z

        def suffix_sums(ch):
            (r0, r1, nk), h = ch
            z = z2[ch]
            x = jnp.maximum(z, 0.0) + jnp.log(1.0 + jnp.exp2(-jnp.abs(z))) * LOG2E
            hi = x.astype(BF16)
            lo = (x - hi.astype(F32)).astype(BF16)
            nl[ch] = x
            later[ch] = _dot(jnp.concatenate([hi, lo], axis=1), ones[nk])

        def weights_times_values(ch):
            (r0, r1, nk), h = ch
            carry = carries[h][r0:r1]
            a = jnp.exp2(z2.pop(ch) - (nl[ch] + later[ch] + carry))
            acc_ref[h, r0:r1, :] += _dot(a.astype(BF16), v_ref[pl.ds(start, nk), pair(h)])
            fresh[ch] = carry + later.pop(ch)[:, 0:1] + nl.pop(ch)[:, 0:1]

        n = len(chains)
        for step in range(n + 2 * SB_STAGE_SKEW):
            if step < n:
                scores(chains[step])
            if 0 <= step - SB_STAGE_SKEW < n:
                suffix_sums(chains[step - SB_STAGE_SKEW])
            if 0 <= step - 2 * SB_STAGE_SKEW < n:
                weights_times_values(chains[step - 2 * SB_STAGE_SKEW])

        out = []
        for h in range(n_heads):
            pieces, row = [], 0
            for job in sorted(jobs):
                r0, r1, _ = job
                if r0 > row:
                    pieces.append(carries[h][row:r0])
                pieces.append(fresh[(job, h)])
                row = r1
            if row < tq:
                pieces.append(carries[h][row:tq])
            out.append(pieces[0] if len(pieces) == 1 else jnp.concatenate(pieces, axis=0))
        return tuple(out)

    zeros = jnp.zeros((tq, 1), F32)
    carries = block(i, (zeros,) * n_heads, [(0, tq, tk)], True)

    def least(carries):
        rows = functools.reduce(jnp.minimum, carries)
        below = jnp.min(rows[top:]) if top < tq else jnp.float32(SB_STICK_GONE_LOG2)
        return jnp.min(rows[:top]), below

    def walk(state, r1, watch):
        def live(state):
            return jnp.logical_and(state[0] < i, state[watch] < SB_STICK_GONE_LOG2)

        def body(state):
            n, _, _, carries = state
            carries = block(i - 1 - n, carries, [(0, r1, tk)], False)
            return (n + 1,) + least(carries) + (carries,)

        return lax.while_loop(live, body, state)

    state = (jnp.int32(0),) + least(carries) + (carries,)
    state = walk(state, tq, 2)
    walk(state, top, 1)
    for p in range(pairs):
        o_ref[:, p * LANES:(p + 1) * LANES] = jnp.where(
            lane < half, acc_ref[2 * p], acc_ref[2 * p + 1]).astype(o_ref.dtype)


def _sb_attention(sbq, sbkt, sbv, batch, seq):
    t = sbq.shape[0]
    tq, tk = min(SB_TQ, seq), min(SB_TK, seq)
    assert tq == tk and seq % tq == 0
    nq = seq // tq
    pairs = SB_PAIRS_PER_STEP
    width = pairs * LANES
    groups = SB_WIDTH // width
    return pl.pallas_call(
        functools.partial(_sb_kernel, tq=tq, tk=tk, pairs=pairs),
        grid=(batch, groups, nq),
        in_specs=[
            pl.BlockSpec((tq, width), lambda b, p, i: (b * nq + i, p)),
            pl.BlockSpec((width, seq), lambda b, p, i: (p, b)),
            pl.BlockSpec((seq, width), lambda b, p, i: (b, p)),
        ],
        out_specs=pl.BlockSpec((tq, width), lambda b, p, i: (b * nq + i, p)),
        out_shape=jax.ShapeDtypeStruct((t, SB_WIDTH), BF16),
        scratch_shapes=[pltpu.VMEM((2 * pairs, tq, LANES), F32)],
        compiler_params=_params(("parallel", "parallel", "arbitrary")),
        name="sb_attention",
    )(sbq, sbkt, sbv)


def _mlstm_kernel(qt_ref, k_ref, vt_ref, og_ref, grow_ref, brow_ref,
                  ghead_ref, o_ref, c_ref, m_ref, gate_ref, scan_ref, *, n_batch, n_chunks):
    L, d, H = ML_KERNEL_CHUNK, ML_HEAD_DIM, ML_HEADS
    chains = [(b, h) for b in range(n_batch) for h in range(H)]

    @pl.when(pl.program_id(1) == 0)
    def _():
        c_ref[...] = jnp.zeros_like(c_ref)
        m_ref[...] = jnp.zeros_like(m_ref)

    ss = lax.broadcasted_iota(jnp.int32, (L, L), 0)
    tt = lax.broadcasted_iota(jnp.int32, (L, L), 1)
    causal = ss <= tt
    eye = ss == tt
    gate_lane = lax.broadcasted_iota(jnp.int32, (2 * H, L), 1)
    gain_col = [jnp.sum(jnp.where(eye, ghead_ref[:, h * d:(h + 1) * d], 0.0), axis=1, keepdims=True)
                for h in range(H)]

    for b in range(n_batch):
        for c in range(n_chunks):
            lanes = slice(c * L, (c + 1) * L)
            g = grow_ref[b, :, lanes] + brow_ref[...]
            scan = _log_sigmoid(g)
            for k in range(L.bit_length() - 1):
                shifted = pltpu.roll(scan, 1 << k, axis=1)
                scan = scan + jnp.where(gate_lane >= (1 << k), shifted, 0.0)
            gate_ref[b, :, lanes] = g
            scan_ref[b, :, lanes] = scan

    def chunk(ci, ms):
        rows = pl.ds(pl.multiple_of(ci * L, L), L)
        lf_scan, g_row = {}, {}
        for b in range(n_batch):
            g_row[b] = gate_ref[b, :, rows]
            lf_scan[b] = scan_ref[b, :, rows]

        kb, qt, vt, c_st, s_raw, q_c = {}, {}, {}, {}, {}, {}
        for ch in chains:
            b, h = ch
            cols = slice(h * d, (h + 1) * d)
            kb[ch] = k_ref[b, rows, cols]
            qt[ch] = qt_ref[b, cols, rows]
            vt[ch] = vt_ref[b, cols, rows]
            c_st[ch] = c_ref[b * H + h]
            s_raw[ch] = _dot(kb[ch], qt[ch])
            q_c[ch] = _dot(c_st[ch].astype(BF16), qt[ch])

        li_row, bcum, g_col = {}, {}, {}
        for ch in chains:
            b, h = ch
            li_row[ch] = g_row[b][h:h + 1, :]
            bcum[ch] = lf_scan[b][H + h:H + h + 1, :]
            g_col[ch] = jnp.sum(jnp.where(eye, li_row[ch] - bcum[ch], 0.0), axis=1, keepdims=True)

        w_inter, m_t, intra, den, w_state, decay, new_ms = {}, {}, {}, {}, {}, {}, []
        for idx, ch in enumerate(chains):
            m_prev = ms[idx]
            dmat = jnp.where(causal, bcum[ch] + g_col.pop(ch), -jnp.inf)
            inter = bcum[ch] + m_prev
            m_t[ch] = jnp.maximum(jnp.max(dmat, axis=0, keepdims=True), inter)
            sc = s_raw.pop(ch) * jnp.exp(dmat - m_t[ch])
            w_inter[ch] = jnp.exp(inter - m_t[ch])
            den[ch] = jnp.sum(sc, axis=0, keepdims=True)
            intra[ch] = _dot(vt[ch], sc.astype(BF16))
            btot = bcum[ch][:, L - 1:L]
            m_new = m_t[ch][:, L - 1:L]
            w_state[ch] = jnp.exp(btot - bcum.pop(ch) + li_row.pop(ch) - m_new)
            decay[ch] = jnp.exp(btot + m_prev - m_new)
            new_ms.append(m_new)

        for ch in chains:
            b, h = ch
            w = w_state.pop(ch)
            lhs = jnp.concatenate([vt.pop(ch).astype(F32) * w,
                                   jnp.broadcast_to(w, (ML_STATE_PAD, L))], axis=0)
            c_ref[b * H + h] = decay.pop(ch) * c_st.pop(ch) + _dot(lhs.astype(BF16), kb.pop(ch))

        for ch in chains:
            b, h = ch
            cols = slice(h * d, (h + 1) * d)
            readout = q_c.pop(ch)
            num = intra.pop(ch) + w_inter[ch] * readout[0:d]
            total = den.pop(ch) + w_inter.pop(ch) * readout[d:d + 1]
            hval = num / jnp.maximum(jnp.abs(total), jnp.exp(-m_t.pop(ch)))
            ssq = jnp.mean(hval * hval, axis=0, keepdims=True)
            hn = (hval * lax.rsqrt(ssq + EPS) * gain_col[h]).T
            og = og_ref[b, rows, cols]
            o_ref[b, rows, cols] = (hn * jax.nn.sigmoid(og)).astype(o_ref.dtype)
        return tuple(new_ms)

    ms = tuple(m_ref[i] for i in range(len(chains)))
    ms = lax.fori_loop(0, n_chunks, chunk, ms)
    for i, m in enumerate(ms):
        m_ref[i] = m


def _mlstm(mlqt, mlk, mlvt, mlo, grow, brow, ghead):
    batch, seq, width = mlk.shape
    nb = min(ML_BATCH_PER_STEP, batch)
    sblk = min(ML_SEQ_BLOCK, seq)
    assert batch % nb == 0 and seq % sblk == 0 and sblk % ML_KERNEL_CHUNK == 0
    assert ML_KERNEL_CHUNK == ML_HEAD_DIM
    chunks = sblk // ML_KERNEL_CHUNK
    n_gate = 2 * ML_HEADS
    seq_blk = pl.BlockSpec((nb, sblk, width), lambda b, s: (b, s, 0))
    seq_blk_t = pl.BlockSpec((nb, width, sblk), lambda b, s: (b, 0, s))
    fixed2 = lambda b, s: (0, 0)
    return pl.pallas_call(
        functools.partial(_mlstm_kernel, n_batch=nb, n_chunks=chunks),
        grid=(batch // nb, seq // sblk),
        in_specs=[
            seq_blk_t, seq_blk, seq_blk_t, seq_blk,
            pl.BlockSpec((nb, n_gate, sblk), lambda b, s: (b, 0, s)),
            pl.BlockSpec(brow.shape, fixed2),
            pl.BlockSpec(ghead.shape, fixed2),
        ],
        out_specs=seq_blk,
        out_shape=jax.ShapeDtypeStruct((batch, seq, width), BF16),
        scratch_shapes=[pltpu.VMEM((nb * ML_HEADS, ML_HEAD_DIM + ML_STATE_PAD, ML_HEAD_DIM), F32),
                        pltpu.VMEM((nb * ML_HEADS, 1, 1), F32),
                        pltpu.VMEM((nb, n_gate, sblk), F32),
                        pltpu.VMEM((nb, n_gate, sblk), F32)],
        compiler_params=_params(("parallel", "arbitrary")),
        name="mlstm",
    )(mlqt, mlk, mlvt, mlo, grow, brow, ghead)


def _memkv_kernel(mem_ref, gm_ref, wk_ref, wv_ref, gk_ref, k_ref, v_ref):
    mn = _rms(mem_ref[...], gm_ref[...]).astype(BF16)
    k = _dot(mn, wk_ref[...])
    hd = k.shape[1] // X_HEADS
    for h in range(X_HEADS):
        k_ref[:, h * hd:(h + 1) * hd] = _rms(k[:, h * hd:(h + 1) * hd], gk_ref[...]).astype(BF16)
    v_ref[...] = _dot(mn, wv_ref[...]).astype(BF16)


def _memkv(mem2d, gm, wk, wv, gk, batch, layer):
    rows, d = mem2d.shape
    m = rows // batch
    blk = pl.BlockSpec((m, d), lambda b: (b, 0))
    fixed = lambda b: (0, 0)
    return pl.pallas_call(
        _memkv_kernel,
        grid=(batch,),
        in_specs=[blk, pl.BlockSpec((1, d), fixed), _layer_weight((d, d), layer),
                  _layer_weight((d, d), layer), pl.BlockSpec((1, d // X_HEADS), fixed)],
        out_specs=[blk, blk],
        out_shape=[jax.ShapeDtypeStruct((rows, d), BF16)] * 2,
        compiler_params=_params(("parallel",)),
        name="memkv",
    )(mem2d, gm.reshape(1, d), wk, wv, gk.reshape(1, d // X_HEADS))


def _post_kernel(x_ref, sb_ref, ml_ref, wo_ref, gx_ref, wq_ref, gq_ref, k_ref, v_ref, wxo_ref,
                 o_ref):
    tm, d = x_ref.shape
    hd = d // X_HEADS
    heads = [slice(h * hd, (h + 1) * hd) for h in range(X_HEADS)]
    parts = [slice(r, r + tm // POST_ROW_PARTS) for r in range(0, tm, tm // POST_ROW_PARTS)]
    x1, q, s, o = {}, {}, {}, {}
    for i, rows in enumerate(parts):
        x1[i] = (x_ref[rows, :] + _dot(sb_ref[rows, :], wo_ref[0:SB_WIDTH, :])
                 + _dot(ml_ref[rows, :], wo_ref[SB_WIDTH:SB_WIDTH + ML_WIDTH, :]))
    for i in range(len(parts)):
        q[i] = _dot(_rms(x1[i], gx_ref[...]).astype(BF16), wq_ref[...])
    for i in range(len(parts)):
        for h, cols in enumerate(heads):
            qn = _rms(q[i][:, cols], gq_ref[...]).astype(BF16)
            s[i, h] = _dot_nt(qn, k_ref[:, cols]) * (hd ** -0.5)
    for i in range(len(parts)):
        outs = []
        for h, cols in enumerate(heads):
            e = jnp.exp(s[i, h] - jnp.max(s[i, h], axis=-1, keepdims=True))
            part = _dot(e.astype(BF16), v_ref[:, cols]) / jnp.sum(e, axis=-1, keepdims=True)
            outs.append(part.astype(BF16))
        o[i] = jnp.concatenate(outs, axis=1)
    for i, rows in enumerate(parts):
        o_ref[rows, :] = x1[i] + _dot(o[i], wxo_ref[...])


def _post(x, sb, ml, wo, gx, wxq, gq, kmem, vmem, wxo, seq, layer):
    t, d = x.shape
    tm = min(POST_TM, seq)
    assert seq % tm == 0 and tm % POST_ROW_PARTS == 0
    per_seq = seq // tm
    m = kmem.shape[0] // (t // seq)
    row = lambda i: (i, 0)
    fixed = lambda i: (0, 0)
    mem_blk = pl.BlockSpec((m, d), lambda i: (i // per_seq, 0))
    return pl.pallas_call(
        _post_kernel,
        grid=(t // tm,),
        in_specs=[
            pl.BlockSpec((tm, d), row),
            pl.BlockSpec((tm, SB_WIDTH), row),
            pl.BlockSpec((tm, ML_WIDTH), row),
            _layer_weight((d, d), layer),
            pl.BlockSpec((1, d), fixed),
            _layer_weight((d, d), layer),
            pl.BlockSpec((1, d // X_HEADS), fixed),
            mem_blk, mem_blk,
            _layer_weight((d, d), layer),
        ],
        out_specs=pl.BlockSpec((tm, d), row),
        out_shape=jax.ShapeDtypeStruct((t, d), F32),
        compiler_params=_params(("parallel",)),
        name="post",
    )(x, sb, ml, wo, gx.reshape(1, d), wxq, gq.reshape(1, d // X_HEADS), kmem, vmem, wxo)


def _layer(x, mem2d, batch, seq, layer, p):
    l = layer
    x = _ffn(x, p["g_ff1"][l], p["w_ff1_gate"], p["w_ff1_up"], p["w_ff1_down"], l)

    s, w = SB_WIDTH, ML_WIDTH
    n_gate = 2 * ML_HEADS
    sbq, sbkt, sbv, mlqt, mlk, mlvt, mlo, gates = _proj(
        x, p["g_mix"][l], p["w_in"], p["w_gate_t"], p["w_conv"][l],
        p["b_conv"][l].reshape(1, 2 * w), seq, l)

    sb = _sb_attention(sbq, sbkt, sbv, batch, seq)

    seq3 = lambda a: a.reshape(batch, seq, w)
    ml = _mlstm(mlqt, seq3(mlk), mlvt, seq3(mlo), gates, p["b_gate"][l].reshape(n_gate, 1),
                p["g_mlstm_head"][l].reshape(1, w)).reshape(batch * seq, w)

    kmem, vmem = _memkv(mem2d, p["g_mem"][l], p["w_xk"], p["w_xv"], p["g_knorm"][l], batch, l)
    x = _post(x, sb, ml, p["w_out"], p["g_xattn"][l], p["w_xq"], p["g_qnorm"][l], kmem, vmem,
              p["w_xo"], seq, l)
    x = _ffn(x, p["g_ff2"][l], p["w_ff2_gate"], p["w_ff2_up"], p["w_ff2_down"], l)
    return x


def _transposed_gate_columns(w_in):
    return w_in[:, :, 3 * SB_WIDTH + 4 * ML_WIDTH:].transpose(0, 2, 1)


_MATMUL_WEIGHTS = ("w_in", "w_out", "w_xq", "w_xk", "w_xv", "w_xo")


def kernel(x, mem, g_ff1, w_ff1_gate, w_ff1_up, w_ff1_down, g_mix, w_in, b_gate, w_conv, b_conv, g_mlstm_head, w_out, g_xattn, g_mem, w_xq, w_xk, w_xv, g_qnorm, g_knorm, w_xo, g_ff2, w_ff2_gate, w_ff2_up, w_ff2_down):
    batch, seq, d = x.shape
    depth = g_ff1.shape[0]
    p = dict(g_ff1=g_ff1, w_ff1_gate=w_ff1_gate, w_ff1_up=w_ff1_up, w_ff1_down=w_ff1_down,
             g_mix=g_mix, w_in=w_in, b_gate=b_gate, w_conv=w_conv, b_conv=b_conv,
             g_mlstm_head=g_mlstm_head, w_out=w_out, g_xattn=g_xattn, g_mem=g_mem, w_xq=w_xq,
             w_xk=w_xk, w_xv=w_xv, g_qnorm=g_qnorm, g_knorm=g_knorm, w_xo=w_xo, g_ff2=g_ff2,
             w_ff2_gate=w_ff2_gate, w_ff2_up=w_ff2_up, w_ff2_down=w_ff2_down)
    for name in _MATMUL_WEIGHTS:
        p[name] = p[name].astype(BF16)
    p["w_gate_t"] = _transposed_gate_columns(p["w_in"])
    h = x.reshape(batch * seq, d)
    mem2d = mem.reshape(batch * mem.shape[1], d)
    for l in range(depth):
        h = _layer(h, mem2d, batch, seq, l, p)
    return h.reshape(batch, seq, d)
```

```python
import functools

import jax
import jax.numpy as jnp
from jax import lax
from jax.experimental import pallas as pl
from jax.experimental.pallas import tpu as pltpu

F32 = jnp.float32
BF16 = jnp.bfloat16

EPS = 1e-6
LOG2E = 1.4426950408889634
SB_HEADS = 8
SB_HEAD_DIM = 64
SB_WIDTH = SB_HEADS * SB_HEAD_DIM
ML_HEADS = 4
ML_HEAD_DIM = 128
ML_WIDTH = ML_HEADS * ML_HEAD_DIM
ML_KERNEL_CHUNK = 128
ML_STATE_PAD = 16
CONV_WIDTH = 4
X_HEADS = 4

V7X_VMEM_BYTES = 64 * 1024 * 1024
VMEM_LIMIT_BYTES = V7X_VMEM_BYTES - 8 * 1024 * 1024
LANES = 128
SUBLANES = 8

FFN_TM = 1024
FFN_TF = 256
PROJ_TM = 1024
POST_TM = 1024
POST_ROW_PARTS = 4
SB_TQ = 256
SB_TK = 256
SB_PAIRS_PER_STEP = 4
SB_STAGE_SKEW = 1
SB_STICK_GONE_LOG2 = 152.0
SB_MASKED_LOGIT = -1e30
SB_TOP_ROWS = 176
ML_BATCH_PER_STEP = 4
ML_SEQ_BLOCK = 512


def _params(semantics):
    return pltpu.CompilerParams(dimension_semantics=semantics,
                                vmem_limit_bytes=VMEM_LIMIT_BYTES)


def _rms(x, g):
    return x * lax.rsqrt(jnp.mean(x * x, axis=-1, keepdims=True) + EPS) * g


def _dot(a, b):
    return jnp.dot(a, b, preferred_element_type=F32)


def _dot_nt(a, b):
    return lax.dot_general(a, b, (((1,), (1,)), ((), ())), preferred_element_type=F32)


def _log_sigmoid(x):
    return -(jnp.maximum(-x, 0.0) + jnp.log1p(jnp.exp(-jnp.abs(x))))


def _layer_weight(shape, layer):
    return pl.BlockSpec((None,) + shape, lambda *_: (layer, 0, 0), pipeline_mode=pl.Buffered(1))


def _ffn_kernel(x_ref, g_ref, wg_hbm, wu_hbm, wd_hbm, o_ref,
                wg_ref, wu_ref, wd_ref, stage_g, stage_u, stage_d, sems, *, tf, layer):
    n_chunks = wg_ref.shape[1] // tf
    chunk = lambda c: slice(c * tf, (c + 1) * tf)

    @pl.when(pl.program_id(0) == 0)
    def _():
        def copies(c, slot):
            return (pltpu.make_async_copy(wg_hbm.at[layer, :, chunk(c)], stage_g.at[slot],
                                          sems.at[0, slot]),
                    pltpu.make_async_copy(wu_hbm.at[layer, :, chunk(c)], stage_u.at[slot],
                                          sems.at[1, slot]),
                    pltpu.make_async_copy(wd_hbm.at[layer, chunk(c), :], stage_d.at[slot],
                                          sems.at[2, slot]))

        for copy in copies(0, 0):
            copy.start()
        for c in range(n_chunks):
            slot = c % 2
            if c + 1 < n_chunks:
                for copy in copies(c + 1, 1 - slot):
                    copy.start()
            for copy in copies(c, slot):
                copy.wait()
            wg_ref[:, chunk(c)] = stage_g[slot].astype(BF16)
            wu_ref[:, chunk(c)] = stage_u[slot].astype(BF16)
            wd_ref[chunk(c), :] = stage_d[slot].astype(BF16)

    x = x_ref[...]
    xn = _rms(x, g_ref[...]).astype(BF16)
    hidden = []
    for c in range(n_chunks):
        gate = _dot(xn, wg_ref[:, chunk(c)])
        up = _dot(xn, wu_ref[:, chunk(c)])
        hidden.append((gate * jax.nn.sigmoid(gate) * up).astype(BF16))
    o_ref[...] = x + 0.5 * _dot(jnp.concatenate(hidden, axis=1), wd_ref[...])


def _ffn(x, g, wg, wu, wd, layer):
    t, d = x.shape
    f = wg.shape[2]
    tm, tf = min(FFN_TM, t), min(FFN_TF, f)
    assert t % tm == 0 and f % tf == 0
    hbm = pl.BlockSpec(memory_space=pl.ANY)
    return pl.pallas_call(
        functools.partial(_ffn_kernel, tf=tf, layer=layer),
        grid=(t // tm,),
        in_specs=[
            pl.BlockSpec((tm, d), lambda i: (i, 0)),
            pl.BlockSpec((1, d), lambda i: (0, 0)),
            hbm, hbm, hbm,
        ],
        out_specs=pl.BlockSpec((tm, d), lambda i: (i, 0)),
        out_shape=jax.ShapeDtypeStruct((t, d), F32),
        scratch_shapes=[
            pltpu.VMEM((d, f), BF16), pltpu.VMEM((d, f), BF16), pltpu.VMEM((f, d), BF16),
            pltpu.VMEM((2, d, tf), F32), pltpu.VMEM((2, d, tf), F32), pltpu.VMEM((2, tf, d), F32),
            pltpu.SemaphoreType.DMA((3, 2)),
        ],
        compiler_params=_params(("arbitrary",)),
        name="ffn",
    )(x, g.reshape(1, d), wg, wu, wd)


def _proj_kernel(x_ref, xprev_ref, g_ref, win_ref, wgate_ref, wconv_ref, bconv_ref,
                 sbq_ref, sbkt_ref, sbv_ref, mlqt_ref, mlk_ref, mlvt_ref, mlo_ref, gates_ref,
                 *, tiles_per_seq):
    tm = x_ref.shape[0]
    s, w = SB_WIDTH, ML_WIDTH
    ml0 = 3 * s
    wq_ref = win_ref.at[:, 0:s]
    wk_ref = win_ref.at[:, s:2 * s]
    wv_ref = win_ref.at[:, 2 * s:3 * s]
    wml_ref = win_ref.at[:, ml0:ml0 + 2 * w]
    wmlv_ref = win_ref.at[:, ml0 + 2 * w:ml0 + 3 * w]
    wmlo_ref = win_ref.at[:, ml0 + 3 * w:ml0 + 4 * w]
    u = _rms(x_ref[...], g_ref[...]).astype(BF16)
    u_prev = _rms(xprev_ref[...], g_ref[...]).astype(BF16)
    first = pl.program_id(0) % tiles_per_seq == 0

    def conv_silu(c0, c1):
        cur = _dot(u, wml_ref[:, c0:c1])
        prev = jnp.where(first, 0.0, _dot(u_prev, wml_ref[:, c0:c1]))
        ext = jnp.concatenate([prev, cur], axis=0)
        y = bconv_ref[:, c0:c1]
        for j in range(CONV_WIDTH):
            back = CONV_WIDTH - 1 - j
            window = pltpu.roll(ext, back, axis=0) if back else ext
            y = y + window[SUBLANES:, :] * wconv_ref[j:j + 1, c0:c1]
        return y * jax.nn.sigmoid(y)

    half = w // 2
    q_scale = ML_HEAD_DIM ** -0.5
    mlqt_ref[0:half, :] = (conv_silu(0, half) * q_scale).T.astype(BF16)
    sbq_ref[...] = (_dot(u, wq_ref[...]) * (LOG2E * SB_HEAD_DIM ** -0.5)).astype(BF16)
    mlqt_ref[half:w, :] = (conv_silu(half, w) * q_scale).T.astype(BF16)
    sbkt_ref[...] = _dot(u, wk_ref[...]).T.astype(BF16)
    mlk_ref[:, 0:half] = conv_silu(w, w + half).astype(BF16)
    sbv_ref[...] = _dot(u, wv_ref[...]).astype(BF16)
    mlk_ref[:, half:w] = conv_silu(w + half, 2 * w).astype(BF16)
    mlvt_ref[...] = _dot(u, wmlv_ref[...]).T.astype(BF16)
    mlo_ref[...] = _dot(u, wmlo_ref[...])
    gates_ref[...] = _dot_nt(wgate_ref[...], u)


def _proj(x, g, w_in, w_gate_t, wconv, bconv, seq, layer):
    t, d = x.shape
    tm = min(PROJ_TM, seq)
    assert seq % tm == 0 and tm % SUBLANES == 0
    tiles_per_seq = seq // tm
    row = lambda i: (i, 0)
    fixed = lambda i: (0, 0)
    whole = lambda a: pl.BlockSpec(a.shape, fixed, pipeline_mode=pl.Buffered(1))
    seq_t = lambda rows: pl.BlockSpec((None, rows, tm),
                                      lambda i: (i // tiles_per_seq, 0, i % tiles_per_seq))
    return pl.pallas_call(
        functools.partial(_proj_kernel, tiles_per_seq=tiles_per_seq),
        grid=(t // tm,),
        in_specs=[
            pl.BlockSpec((tm, d), row),
            pl.BlockSpec((SUBLANES, d), lambda i: (jnp.maximum(i * (tm // SUBLANES) - 1, 0), 0)),
            pl.BlockSpec((1, d), fixed),
            _layer_weight(w_in.shape[1:], layer), _layer_weight(w_gate_t.shape[1:], layer),
            whole(wconv), whole(bconv),
        ],
        out_specs=[
            pl.BlockSpec((tm, SB_WIDTH), row),
            pl.BlockSpec((SB_WIDTH, tm), lambda i: (0, i)),
            pl.BlockSpec((tm, SB_WIDTH), row),
            seq_t(ML_WIDTH),
            pl.BlockSpec((tm, ML_WIDTH), row),
            seq_t(ML_WIDTH),
            pl.BlockSpec((tm, ML_WIDTH), row),
            seq_t(2 * ML_HEADS),
        ],
        out_shape=[
            jax.ShapeDtypeStruct((t, SB_WIDTH), BF16),
            jax.ShapeDtypeStruct((SB_WIDTH, t), BF16),
            jax.ShapeDtypeStruct((t, SB_WIDTH), BF16),
            jax.ShapeDtypeStruct((t // seq, ML_WIDTH, seq), BF16),
            jax.ShapeDtypeStruct((t, ML_WIDTH), BF16),
            jax.ShapeDtypeStruct((t // seq, ML_WIDTH, seq), BF16),
            jax.ShapeDtypeStruct((t, ML_WIDTH), F32),
            jax.ShapeDtypeStruct((t // seq, 2 * ML_HEADS, seq), F32),
        ],
        compiler_params=_params(("parallel",)),
        name="proj",
    )(x, x, g.reshape(1, d), w_in, w_gate_t, wconv, bconv)


def _sb_kernel(q_ref, kt_ref, v_ref, o_ref, acc_ref, *, tq, tk, pairs):
    i = pl.program_id(2)
    half = SB_HEAD_DIM
    n_heads = 2 * pairs
    top = min(SB_TOP_ROWS, tq)
    lane = lax.broadcasted_iota(jnp.int32, (tq, LANES), 1)
    q_heads = []
    for p in range(pairs):
        q = q_ref[:, p * LANES:(p + 1) * LANES]
        zero = jnp.zeros_like(q)
        q_heads += [jnp.where(lane < half, q, zero), jnp.where(lane >= half, q, zero)]
    pair = lambda h: slice((h // 2) * LANES, (h // 2 + 1) * LANES)

    def suffix_ones(n):
        r = lax.broadcasted_iota(jnp.int32, (2 * n, n), 0)
        c = lax.broadcasted_iota(jnp.int32, (2 * n, n), 1)
        return jnp.where((r & (n - 1)) > c, 1.0, 0.0).astype(BF16)

    ones = {tk: suffix_ones(tk)}

    acc_ref[...] = jnp.zeros_like(acc_ref)

    def block(kb, carries, jobs, masked):
        start = pl.multiple_of(kb * tk, tk)
        chains = [(job, h) for job in jobs for h in range(n_heads)]
        z2, nl, later, fresh = {}, {}, {}, {}

        def strict(job):
            r0, r1, nk = job
            tr = lax.broadcasted_iota(jnp.int32, (r1 - r0, nk), 0) + r0
            tc = lax.broadcasted_iota(jnp.int32, (r1 - r0, nk), 1)
            return tc < tr

        def scores(ch):
            (r0, r1, nk), h = ch
            z = _dot(q_heads[h][r0:r1], kt_ref[pair(h), pl.ds(start, nk)])
            if masked:
                z = jnp.where(strict(ch[0]), z, SB_MASKED_LOGIT)
            z2[ch] = z

        def suffix_sums(ch):
            (r0, r1, nk), h = ch
            z = z2[ch]
            x = jnp.maximum(z, 0.0) + jnp.log(1.0 + jnp.exp2(-jnp.abs(z))) * LOG2E
            hi = x.astype(BF16)
            lo = (x - hi.astype(F32)).astype(BF16)
            nl[ch] = x
            later[ch] = _dot(jnp.concatenate([hi, lo], axis=1), ones[nk])

        def weights_times_values(ch):
            (r0, r1, nk), h = ch
            carry = carries[h][r0:r1]
            a = jnp.exp2(z2.pop(ch) - (nl[ch] + later[ch] + carry))
            acc_ref[h, r0:r1, :] += _dot(a.astype(BF16), v_ref[pl.ds(start, nk), pair(h)])
            fresh[ch] = carry + later.pop(ch)[:, 0:1] + nl.pop(ch)[:, 0:1]

        n = len(chains)
        for step in range(n + 2 * SB_STAGE_SKEW):
            if step < n:
                scores(chains[step])
            if 0 <= step - SB_STAGE_SKEW < n:
                suffix_sums(chains[step - SB_STAGE_SKEW])
            if 0 <= step - 2 * SB_STAGE_SKEW < n:
                weights_times_values(chains[step - 2 * SB_STAGE_SKEW])

        out = []
        for h in range(n_heads):
            pieces, row = [], 0
            for job in sorted(jobs):
                r0, r1, _ = job
                if r0 > row:
                    pieces.append(carries[h][row:r0])
                pieces.append(fresh[(job, h)])
                row = r1
            if row < tq:
                pieces.append(carries[h][row:tq])
            out.append(pieces[0] if len(pieces) == 1 else jnp.concatenate(pieces, axis=0))
        return tuple(out)

    zeros = jnp.zeros((tq, 1), F32)
    carries = block(i, (zeros,) * n_heads, [(0, tq, tk)], True)

    def least(carries):
        rows = functools.reduce(jnp.minimum, carries)
        below = jnp.min(rows[top:]) if top < tq else jnp.float32(SB_STICK_GONE_LOG2)
        return jnp.min(rows[:top]), below

    def walk(state, r1, watch):
        def live(state):
            return jnp.logical_and(state[0] < i, state[watch] < SB_STICK_GONE_LOG2)

        def body(state):
            n, _, _, carries = state
            carries = block(i - 1 - n, carries, [(0, r1, tk)], False)
            return (n + 1,) + least(carries) + (carries,)

        return lax.while_loop(live, body, state)

    state = (jnp.int32(0),) + least(carries) + (carries,)
    state = walk(state, tq, 2)
    walk(state, top, 1)
    for p in range(pairs):
        o_ref[:, p * LANES:(p + 1) * LANES] = jnp.where(
            lane < half, acc_ref[2 * p], acc_ref[2 * p + 1]).astype(o_ref.dtype)


def _sb_attention(sbq, sbkt, sbv, batch, seq):
    t = sbq.shape[0]
    tq, tk = min(SB_TQ, seq), min(SB_TK, seq)
    assert tq == tk and seq % tq == 0
    nq = seq // tq
    pairs = SB_PAIRS_PER_STEP
    width = pairs * LANES
    groups = SB_WIDTH // width
    return pl.pallas_call(
        functools.partial(_sb_kernel, tq=tq, tk=tk, pairs=pairs),
        grid=(batch, groups, nq),
        in_specs=[
            pl.BlockSpec((tq, width), lambda b, p, i: (b * nq + i, p)),
            pl.BlockSpec((width, seq), lambda b, p, i: (p, b)),
            pl.BlockSpec((seq, width), lambda b, p, i: (b, p)),
        ],
        out_specs=pl.BlockSpec((tq, width), lambda b, p, i: (b * nq + i, p)),
        out_shape=jax.ShapeDtypeStruct((t, SB_WIDTH), BF16),
        scratch_shapes=[pltpu.VMEM((2 * pairs, tq, LANES), F32)],
        compiler_params=_params(("parallel", "parallel", "arbitrary")),
        name="sb_attention",
    )(sbq, sbkt, sbv)


def _mlstm_kernel(qt_ref, k_ref, vt_ref, og_ref, grow_ref, brow_ref,
                  ghead_ref, o_ref, c_ref, m_ref, gate_ref, scan_ref, *, n_batch, n_chunks):
    L, d, H = ML_KERNEL_CHUNK, ML_HEAD_DIM, ML_HEADS
    chains = [(b, h) for b in range(n_batch) for h in range(H)]

    @pl.when(pl.program_id(1) == 0)
    def _():
        c_ref[...] = jnp.zeros_like(c_ref)
        m_ref[...] = jnp.zeros_like(m_ref)

    ss = lax.broadcasted_iota(jnp.int32, (L, L), 0)
    tt = lax.broadcasted_iota(jnp.int32, (L, L), 1)
    causal = ss <= tt
    eye = ss == tt
    gate_lane = lax.broadcasted_iota(jnp.int32, (2 * H, L), 1)
    gain_col = [jnp.sum(jnp.where(eye, ghead_ref[:, h * d:(h + 1) * d], 0.0), axis=1, keepdims=True)
                for h in range(H)]

    for b in range(n_batch):
        for c in range(n_chunks):
            lanes = slice(c * L, (c + 1) * L)
            g = grow_ref[b, :, lanes] + brow_ref[...]
            scan = _log_sigmoid(g)
            for k in range(L.bit_length() - 1):
                shifted = pltpu.roll(scan, 1 << k, axis=1)
                scan = scan + jnp.where(gate_lane >= (1 << k), shifted, 0.0)
            gate_ref[b, :, lanes] = g
            scan_ref[b, :, lanes] = scan

    def chunk(ci, ms):
        rows = pl.ds(pl.multiple_of(ci * L, L), L)
        lf_scan, g_row = {}, {}
        for b in range(n_batch):
            g_row[b] = gate_ref[b, :, rows]
            lf_scan[b] = scan_ref[b, :, rows]

        kb, qt, vt, c_st, s_raw, q_c = {}, {}, {}, {}, {}, {}
        for ch in chains:
            b, h = ch
            cols = slice(h * d, (h + 1) * d)
            kb[ch] = k_ref[b, rows, cols]
            qt[ch] = qt_ref[b, cols, rows]
            vt[ch] = vt_ref[b, cols, rows]
            c_st[ch] = c_ref[b * H + h]
            s_raw[ch] = _dot(kb[ch], qt[ch])
            q_c[ch] = _dot(c_st[ch].astype(BF16), qt[ch])

        li_row, bcum, g_col = {}, {}, {}
        for ch in chains:
            b, h = ch
            li_row[ch] = g_row[b][h:h + 1, :]
            bcum[ch] = lf_scan[b][H + h:H + h + 1, :]
            g_col[ch] = jnp.sum(jnp.where(eye, li_row[ch] - bcum[ch], 0.0), axis=1, keepdims=True)

        w_inter, m_t, intra, den, w_state, decay, new_ms = {}, {}, {}, {}, {}, {}, []
        for idx, ch in enumerate(chains):
            m_prev = ms[idx]
            dmat = jnp.where(causal, bcum[ch] + g_col.pop(ch), -jnp.inf)
            inter = bcum[ch] + m_prev
            m_t[ch] = jnp.maximum(jnp.max(dmat, axis=0, keepdims=True), inter)
            sc = s_raw.pop(ch) * jnp.exp(dmat - m_t[ch])
            w_inter[ch] = jnp.exp(inter - m_t[ch])
            den[ch] = jnp.sum(sc, axis=0, keepdims=True)
            intra[ch] = _dot(vt[ch], sc.astype(BF16))
            btot = bcum[ch][:, L - 1:L]
            m_new = m_t[ch][:, L - 1:L]
            w_state[ch] = jnp.exp(btot - bcum.pop(ch) + li_row.pop(ch) - m_new)
            decay[ch] = jnp.exp(btot + m_prev - m_new)
            new_ms.append(m_new)

        for ch in chains:
            b, h = ch
            w = w_state.pop(ch)
            lhs = jnp.concatenate([vt.pop(ch).astype(F32) * w,
                                   jnp.broadcast_to(w, (ML_STATE_PAD, L))], axis=0)
            c_ref[b * H + h] = decay.pop(ch) * c_st.pop(ch) + _dot(lhs.astype(BF16), kb.pop(ch))

        for ch in chains:
            b, h = ch
            cols = slice(h * d, (h + 1) * d)
            readout = q_c.pop(ch)
            num = intra.pop(ch) + w_inter[ch] * readout[0:d]
            total = den.pop(ch) + w_inter.pop(ch) * readout[d:d + 1]
            hval = num / jnp.maximum(jnp.abs(total), jnp.exp(-m_t.pop(ch)))
            ssq = jnp.mean(hval * hval, axis=0, keepdims=True)
            hn = (hval * lax.rsqrt(ssq + EPS) * gain_col[h]).T
            og = og_ref[b, rows, cols]
            o_ref[b, rows, cols] = (hn * jax.nn.sigmoid(og)).astype(o_ref.dtype)
        return tuple(new_ms)

    ms = tuple(m_ref[i] for i in range(len(chains)))
    ms = lax.fori_loop(0, n_chunks, chunk, ms)
    for i, m in enumerate(ms):
        m_ref[i] = m


def _mlstm(mlqt, mlk, mlvt, mlo, grow, brow, ghead):
    batch, seq, width = mlk.shape
    nb = min(ML_BATCH_PER_STEP, batch)
    sblk = min(ML_SEQ_BLOCK, seq)
    assert batch % nb == 0 and seq % sblk == 0 and sblk % ML_KERNEL_CHUNK == 0
    assert ML_KERNEL_CHUNK == ML_HEAD_DIM
    chunks = sblk // ML_KERNEL_CHUNK
    n_gate = 2 * ML_HEADS
    seq_blk = pl.BlockSpec((nb, sblk, width), lambda b, s: (b, s, 0))
    seq_blk_t = pl.BlockSpec((nb, width, sblk), lambda b, s: (b, 0, s))
    fixed2 = lambda b, s: (0, 0)
    return pl.pallas_call(
        functools.partial(_mlstm_kernel, n_batch=nb, n_chunks=chunks),
        grid=(batch // nb, seq // sblk),
        in_specs=[
            seq_blk_t, seq_blk, seq_blk_t, seq_blk,
            pl.BlockSpec((nb, n_gate, sblk), lambda b, s: (b, 0, s)),
            pl.BlockSpec(brow.shape, fixed2),
            pl.BlockSpec(ghead.shape, fixed2),
        ],
        out_specs=seq_blk,
        out_shape=jax.ShapeDtypeStruct((batch, seq, width), BF16),
        scratch_shapes=[pltpu.VMEM((nb * ML_HEADS, ML_HEAD_DIM + ML_STATE_PAD, ML_HEAD_DIM), F32),
                        pltpu.VMEM((nb * ML_HEADS, 1, 1), F32),
                        pltpu.VMEM((nb, n_gate, sblk), F32),
                        pltpu.VMEM((nb, n_gate, sblk), F32)],
        compiler_params=_params(("parallel", "arbitrary")),
        name="mlstm",
    )(mlqt, mlk, mlvt, mlo, grow, brow, ghead)


def _memkv_kernel(mem_ref, gm_ref, wk_ref, wv_ref, gk_ref, k_ref, v_ref):
    mn = _rms(mem_ref[...], gm_ref[...]).astype(BF16)
    k = _dot(mn, wk_ref[...])
    hd = k.shape[1] // X_HEADS
    for h in range(X_HEADS):
        k_ref[:, h * hd:(h + 1) * hd] = _rms(k[:, h * hd:(h + 1) * hd], gk_ref[...]).astype(BF16)
    v_ref[...] = _dot(mn, wv_ref[...]).astype(BF16)


def _memkv(mem2d, gm, wk, wv, gk, batch, layer):
    rows, d = mem2d.shape
    m = rows // batch
    blk = pl.BlockSpec((m, d), lambda b: (b, 0))
    fixed = lambda b: (0, 0)
    return pl.pallas_call(
        _memkv_kernel,
        grid=(batch,),
        in_specs=[blk, pl.BlockSpec((1, d), fixed), _layer_weight((d, d), layer),
                  _layer_weight((d, d), layer), pl.BlockSpec((1, d // X_HEADS), fixed)],
        out_specs=[blk, blk],
        out_shape=[jax.ShapeDtypeStruct((rows, d), BF16)] * 2,
        compiler_params=_params(("parallel",)),
        name="memkv",
    )(mem2d, gm.reshape(1, d), wk, wv, gk.reshape(1, d // X_HEADS))


def _post_kernel(x_ref, sb_ref, ml_ref, wo_ref, gx_ref, wq_ref, gq_ref, k_ref, v_ref, wxo_ref,
                 o_ref):
    tm, d = x_ref.shape
    hd = d // X_HEADS
    heads = [slice(h * hd, (h + 1) * hd) for h in range(X_HEADS)]
    parts = [slice(r, r + tm // POST_ROW_PARTS) for r in range(0, tm, tm // POST_ROW_PARTS)]
    x1, q, s, o = {}, {}, {}, {}
    for i, rows in enumerate(parts):
        x1[i] = (x_ref[rows, :] + _dot(sb_ref[rows, :], wo_ref[0:SB_WIDTH, :])
                 + _dot(ml_ref[rows, :], wo_ref[SB_WIDTH:SB_WIDTH + ML_WIDTH, :]))
    for i in range(len(parts)):
        q[i] = _dot(_rms(x1[i], gx_ref[...]).astype(BF16), wq_ref[...])
    for i in range(len(parts)):
        for h, cols in enumerate(heads):
            qn = _rms(q[i][:, cols], gq_ref[...]).astype(BF16)
            s[i, h] = _dot_nt(qn, k_ref[:, cols]) * (hd ** -0.5)
    for i in range(len(parts)):
        outs = []
        for h, cols in enumerate(heads):
            e = jnp.exp(s[i, h] - jnp.max(s[i, h], axis=-1, keepdims=True))
            part = _dot(e.astype(BF16), v_ref[:, cols]) / jnp.sum(e, axis=-1, keepdims=True)
            outs.append(part.astype(BF16))
        o[i] = jnp.concatenate(outs, axis=1)
    for i, rows in enumerate(parts):
        o_ref[rows, :] = x1[i] + _dot(o[i], wxo_ref[...])


def _post(x, sb, ml, wo, gx, wxq, gq, kmem, vmem, wxo, seq, layer):
    t, d = x.shape
    tm = min(POST_TM, seq)
    assert seq % tm == 0 and tm % POST_ROW_PARTS == 0
    per_seq = seq // tm
    m = kmem.shape[0] // (t // seq)
    row = lambda i: (i, 0)
    fixed = lambda i: (0, 0)
    mem_blk = pl.BlockSpec((m, d), lambda i: (i // per_seq, 0))
    return pl.pallas_call(
        _post_kernel,
        grid=(t // tm,),
        in_specs=[
            pl.BlockSpec((tm, d), row),
            pl.BlockSpec((tm, SB_WIDTH), row),
            pl.BlockSpec((tm, ML_WIDTH), row),
            _layer_weight((d, d), layer),
            pl.BlockSpec((1, d), fixed),
            _layer_weight((d, d), layer),
            pl.BlockSpec((1, d // X_HEADS), fixed),
            mem_blk, mem_blk,
            _layer_weight((d, d), layer),
        ],
        out_specs=pl.BlockSpec((tm, d), row),
        out_shape=jax.ShapeDtypeStruct((t, d), F32),
        compiler_params=_params(("parallel",)),
        name="post",
    )(x, sb, ml, wo, gx.reshape(1, d), wxq, gq.reshape(1, d // X_HEADS), kmem, vmem, wxo)


def _layer(x, mem2d, batch, seq, layer, p):
    l = layer
    x = _ffn(x, p["g_ff1"][l], p["w_ff1_gate"], p["w_ff1_up"], p["w_ff1_down"], l)

    s, w = SB_WIDTH, ML_WIDTH
    n_gate = 2 * ML_HEADS
    sbq, sbkt, sbv, mlqt, mlk, mlvt, mlo, gates = _proj(
        x, p["g_mix"][l], p["w_in"], p["w_gate_t"], p["w_conv"][l],
        p["b_conv"][l].reshape(1, 2 * w), seq, l)

    sb = _sb_attention(sbq, sbkt, sbv, batch, seq)

    seq3 = lambda a: a.reshape(batch, seq, w)
    ml = _mlstm(mlqt, seq3(mlk), mlvt, seq3(mlo), gates, p["b_gate"][l].reshape(n_gate, 1),
                p["g_mlstm_head"][l].reshape(1, w)).reshape(batch * seq, w)

    kmem, vmem = _memkv(mem2d, p["g_mem"][l], p["w_xk"], p["w_xv"], p["g_knorm"][l], batch, l)
    x = _post(x, sb, ml, p["w_out"], p["g_xattn"][l], p["w_xq"], p["g_qnorm"][l], kmem, vmem,
              p["w_xo"], seq, l)
    x = _ffn(x, p["g_ff2"][l], p["w_ff2_gate"], p["w_ff2_up"], p["w_ff2_down"], l)
    return x


def _split_in_projection(w_in):
    n_wide = 3 * SB_WIDTH + 4 * ML_WIDTH
    return (w_in[:, :, :n_wide].astype(BF16),
            w_in[:, :, n_wide:].transpose(0, 2, 1).astype(BF16))


_MATMUL_WEIGHTS = ("w_out", "w_xq", "w_xk", "w_xv", "w_xo")


def kernel(x, mem, g_ff1, w_ff1_gate, w_ff1_up, w_ff1_down, g_mix, w_in, b_gate, w_conv, b_conv, g_mlstm_head, w_out, g_xattn, g_mem, w_xq, w_xk, w_xv, g_qnorm, g_knorm, w_xo, g_ff2, w_ff2_gate, w_ff2_up, w_ff2_down):
    batch, seq, d = x.shape
    depth = g_ff1.shape[0]
    p = dict(g_ff1=g_ff1, w_ff1_gate=w_ff1_gate, w_ff1_up=w_ff1_up, w_ff1_down=w_ff1_down,
             g_mix=g_mix, w_in=w_in, b_gate=b_gate, w_conv=w_conv, b_conv=b_conv,
             g_mlstm_head=g_mlstm_head, w_out=w_out, g_xattn=g_xattn, g_mem=g_mem, w_xq=w_xq,
             w_xk=w_xk, w_xv=w_xv, g_qnorm=g_qnorm, g_knorm=g_knorm, w_xo=w_xo, g_ff2=g_ff2,
             w_ff2_gate=w_ff2_gate, w_ff2_up=w_ff2_up, w_ff2_down=w_ff2_down)
    for name in _MATMUL_WEIGHTS:
        p[name] = p[name].astype(BF16)
    p["w_in"], p["w_gate_t"] = _split_in_projection(w_in)
    h = x.reshape(batch * seq, d)
    mem2d = mem.reshape(batch * mem.shape[1], d)
    for l in range(depth):
        h = _layer(h, mem2d, batch, seq, l, p)
    return h.reshape(batch, seq, d)
```

```python
import functools

import jax
import jax.numpy as jnp
from jax import lax
from jax.experimental import pallas as pl
from jax.experimental.pallas import tpu as pltpu

F32 = jnp.float32
BF16 = jnp.bfloat16

EPS = 1e-6
LOG2E = 1.4426950408889634
SB_HEADS = 8
SB_HEAD_DIM = 64
SB_WIDTH = SB_HEADS * SB_HEAD_DIM
ML_HEADS = 4
ML_HEAD_DIM = 128
ML_WIDTH = ML_HEADS * ML_HEAD_DIM
ML_KERNEL_CHUNK = 128
ML_STATE_PAD = 16
CONV_WIDTH = 4
X_HEADS = 4

V7X_VMEM_BYTES = 64 * 1024 * 1024
VMEM_LIMIT_BYTES = V7X_VMEM_BYTES - 8 * 1024 * 1024
LANES = 128
SUBLANES = 8

FFN_TM = 1024
FFN_TF = 256
PROJ_TM = 1024
POST_TM = 1024
POST_ROW_PARTS = 4
SB_TQ = 256
SB_TK = 256
SB_PAIRS_PER_STEP = 4
SB_STAGE_SKEW = 1
SB_STICK_GONE_LOG2 = 152.0
SB_MASKED_LOGIT = -1e30
SB_TOP_ROWS = 176
ML_BATCH_PER_STEP = 4
ML_SEQ_BLOCK = 512


def _params(semantics):
    return pltpu.CompilerParams(dimension_semantics=semantics,
                                vmem_limit_bytes=VMEM_LIMIT_BYTES)


def _rms(x, g):
    return x * lax.rsqrt(jnp.mean(x * x, axis=-1, keepdims=True) + EPS) * g


def _dot(a, b):
    return jnp.dot(a, b, preferred_element_type=F32)


def _dot_nt(a, b):
    return lax.dot_general(a, b, (((1,), (1,)), ((), ())), preferred_element_type=F32)


def _log_sigmoid(x):
    return -(jnp.maximum(-x, 0.0) + jnp.log1p(jnp.exp(-jnp.abs(x))))


def _layer_weight(shape, layer):
    return pl.BlockSpec((None,) + shape, lambda *_: (layer, 0, 0), pipeline_mode=pl.Buffered(1))


def _ffn_kernel(x_ref, g_ref, wg_hbm, wu_hbm, wd_hbm, o_ref,
                wg_ref, wu_ref, wd_ref, stage_g, stage_u, stage_d, sems, *, tf, layer):
    n_chunks = wg_ref.shape[1] // tf
    chunk = lambda c: slice(c * tf, (c + 1) * tf)

    @pl.when(pl.program_id(0) == 0)
    def _():
        def copies(c, slot):
            return (pltpu.make_async_copy(wg_hbm.at[layer, :, chunk(c)], stage_g.at[slot],
                                          sems.at[0, slot]),
                    pltpu.make_async_copy(wu_hbm.at[layer, :, chunk(c)], stage_u.at[slot],
                                          sems.at[1, slot]),
                    pltpu.make_async_copy(wd_hbm.at[layer, chunk(c), :], stage_d.at[slot],
                                          sems.at[2, slot]))

        for copy in copies(0, 0):
            copy.start()
        for c in range(n_chunks):
            slot = c % 2
            if c + 1 < n_chunks:
                for copy in copies(c + 1, 1 - slot):
                    copy.start()
            for copy in copies(c, slot):
                copy.wait()
            wg_ref[:, chunk(c)] = stage_g[slot].astype(BF16)
            wu_ref[:, chunk(c)] = stage_u[slot].astype(BF16)
            wd_ref[chunk(c), :] = stage_d[slot].astype(BF16)

    x = x_ref[...]
    xn = _rms(x, g_ref[...]).astype(BF16)
    hidden = []
    for c in range(n_chunks):
        gate = _dot(xn, wg_ref[:, chunk(c)])
        up = _dot(xn, wu_ref[:, chunk(c)])
        hidden.append((gate * jax.nn.sigmoid(gate) * up).astype(BF16))
    o_ref[...] = x + 0.5 * _dot(jnp.concatenate(hidden, axis=1), wd_ref[...])


def _ffn(x, g, wg, wu, wd, layer):
    t, d = x.shape
    f = wg.shape[2]
    tm, tf = min(FFN_TM, t), min(FFN_TF, f)
    assert t % tm == 0 and f % tf == 0
    hbm = pl.BlockSpec(memory_space=pl.ANY)
    return pl.pallas_call(
        functools.partial(_ffn_kernel, tf=tf, layer=layer),
        grid=(t // tm,),
        in_specs=[
            pl.BlockSpec((tm, d), lambda i: (i, 0)),
            pl.BlockSpec((1, d), lambda i: (0, 0)),
            hbm, hbm, hbm,
        ],
        out_specs=pl.BlockSpec((tm, d), lambda i: (i, 0)),
        out_shape=jax.ShapeDtypeStruct((t, d), F32),
        scratch_shapes=[
            pltpu.VMEM((d, f), BF16), pltpu.VMEM((d, f), BF16), pltpu.VMEM((f, d), BF16),
            pltpu.VMEM((2, d, tf), F32), pltpu.VMEM((2, d, tf), F32), pltpu.VMEM((2, tf, d), F32),
            pltpu.SemaphoreType.DMA((3, 2)),
        ],
        compiler_params=_params(("arbitrary",)),
        name="ffn",
    )(x, g.reshape(1, d), wg, wu, wd)


def _proj_kernel(x_ref, xprev_ref, g_ref, wint_ref, wconv_ref, bconv_ref,
                 sbq_ref, sbkt_ref, sbv_ref, mlqt_ref, mlk_ref, mlvt_ref, mlo_ref, gates_ref,
                 *, tiles_per_seq):
    tm = x_ref.shape[0]
    s, w = SB_WIDTH, ML_WIDTH
    ml0 = 3 * s
    wq_ref = wint_ref.at[0:s, :]
    wk_ref = wint_ref.at[s:2 * s, :]
    wv_ref = wint_ref.at[2 * s:3 * s, :]
    wml_ref = wint_ref.at[ml0:ml0 + 2 * w, :]
    wmlv_ref = wint_ref.at[ml0 + 2 * w:ml0 + 3 * w, :]
    wmlo_ref = wint_ref.at[ml0 + 3 * w:ml0 + 4 * w, :]
    wgate_ref = wint_ref.at[ml0 + 4 * w:ml0 + 4 * w + 2 * ML_HEADS, :]
    u = _rms(x_ref[...], g_ref[...]).astype(BF16)
    u_prev = _rms(xprev_ref[...], g_ref[...]).astype(BF16)
    first = pl.program_id(0) % tiles_per_seq == 0

    def conv_silu(c0, c1):
        cur = _dot_nt(u, wml_ref[c0:c1, :])
        prev = jnp.where(first, 0.0, _dot_nt(u_prev, wml_ref[c0:c1, :]))
        ext = jnp.concatenate([prev, cur], axis=0)
        y = bconv_ref[:, c0:c1]
        for j in range(CONV_WIDTH):
            back = CONV_WIDTH - 1 - j
            window = pltpu.roll(ext, back, axis=0) if back else ext
            y = y + window[SUBLANES:, :] * wconv_ref[j:j + 1, c0:c1]
        return y * jax.nn.sigmoid(y)

    half = w // 2
    q_scale = ML_HEAD_DIM ** -0.5
    mlqt_ref[0:half, :] = (conv_silu(0, half) * q_scale).T.astype(BF16)
    sbq_ref[...] = (_dot_nt(u, wq_ref[...]) * (LOG2E * SB_HEAD_DIM ** -0.5)).astype(BF16)
    mlqt_ref[half:w, :] = (conv_silu(half, w) * q_scale).T.astype(BF16)
    sbkt_ref[...] = _dot_nt(u, wk_ref[...]).T.astype(BF16)
    mlk_ref[:, 0:half] = conv_silu(w, w + half).astype(BF16)
    sbv_ref[...] = _dot_nt(u, wv_ref[...]).astype(BF16)
    mlk_ref[:, half:w] = conv_silu(w + half, 2 * w).astype(BF16)
    mlvt_ref[...] = _dot_nt(u, wmlv_ref[...]).T.astype(BF16)
    mlo_ref[...] = _dot_nt(u, wmlo_ref[...])
    gates_ref[...] = _dot_nt(wgate_ref[...], u)


def _proj(x, g, w_in_t, wconv, bconv, seq, layer):
    t, d = x.shape
    tm = min(PROJ_TM, seq)
    assert seq % tm == 0 and tm % SUBLANES == 0
    tiles_per_seq = seq // tm
    row = lambda i: (i, 0)
    fixed = lambda i: (0, 0)
    whole = lambda a: pl.BlockSpec(a.shape, fixed, pipeline_mode=pl.Buffered(1))
    seq_t = lambda rows: pl.BlockSpec((None, rows, tm),
                                      lambda i: (i // tiles_per_seq, 0, i % tiles_per_seq))
    return pl.pallas_call(
        functools.partial(_proj_kernel, tiles_per_seq=tiles_per_seq),
        grid=(t // tm,),
        in_specs=[
            pl.BlockSpec((tm, d), row),
            pl.BlockSpec((SUBLANES, d), lambda i: (jnp.maximum(i * (tm // SUBLANES) - 1, 0), 0)),
            pl.BlockSpec((1, d), fixed),
            _layer_weight(w_in_t.shape[1:], layer),
            whole(wconv), whole(bconv),
        ],
        out_specs=[
            pl.BlockSpec((tm, SB_WIDTH), row),
            pl.BlockSpec((SB_WIDTH, tm), lambda i: (0, i)),
            pl.BlockSpec((tm, SB_WIDTH), row),
            seq_t(ML_WIDTH),
            pl.BlockSpec((tm, ML_WIDTH), row),
            seq_t(ML_WIDTH),
            pl.BlockSpec((tm, ML_WIDTH), row),
            seq_t(2 * ML_HEADS),
        ],
        out_shape=[
            jax.ShapeDtypeStruct((t, SB_WIDTH), BF16),
            jax.ShapeDtypeStruct((SB_WIDTH, t), BF16),
            jax.ShapeDtypeStruct((t, SB_WIDTH), BF16),
            jax.ShapeDtypeStruct((t // seq, ML_WIDTH, seq), BF16),
            jax.ShapeDtypeStruct((t, ML_WIDTH), BF16),
            jax.ShapeDtypeStruct((t // seq, ML_WIDTH, seq), BF16),
            jax.ShapeDtypeStruct((t, ML_WIDTH), F32),
            jax.ShapeDtypeStruct((t // seq, 2 * ML_HEADS, seq), F32),
        ],
        compiler_params=_params(("parallel",)),
        name="proj",
    )(x, x, g.reshape(1, d), w_in_t, wconv, bconv)


def _sb_kernel(q_ref, kt_ref, v_ref, o_ref, acc_ref, *, tq, tk, pairs):
    i = pl.program_id(2)
    half = SB_HEAD_DIM
    n_heads = 2 * pairs
    top = min(SB_TOP_ROWS, tq)
    lane = lax.broadcasted_iota(jnp.int32, (tq, LANES), 1)
    q_heads = []
    for p in range(pairs):
        q = q_ref[:, p * LANES:(p + 1) * LANES]
        zero = jnp.zeros_like(q)
        q_heads += [jnp.where(lane < half, q, zero), jnp.where(lane >= half, q, zero)]
    pair = lambda h: slice((h // 2) * LANES, (h // 2 + 1) * LANES)

    def suffix_ones(n):
        r = lax.broadcasted_iota(jnp.int32, (2 * n, n), 0)
        c = lax.broadcasted_iota(jnp.int32, (2 * n, n), 1)
        return jnp.where((r & (n - 1)) > c, 1.0, 0.0).astype(BF16)

    ones = {tk: suffix_ones(tk)}

    acc_ref[...] = jnp.zeros_like(acc_ref)

    def block(kb, carries, jobs, masked):
        start = pl.multiple_of(kb * tk, tk)
        chains = [(job, h) for job in jobs for h in range(n_heads)]
        z2, nl, later, fresh = {}, {}, {}, {}

        def strict(job):
            r0, r1, nk = job
            tr = lax.broadcasted_iota(jnp.int32, (r1 - r0, nk), 0) + r0
            tc = lax.broadcasted_iota(jnp.int32, (r1 - r0, nk), 1)
            return tc < tr

        def scores(ch):
            (r0, r1, nk), h = ch
            z = _dot(q_heads[h][r0:r1], kt_ref[pair(h), pl.ds(start, nk)])
            if masked:
                z = jnp.where(strict(ch[0]), z, SB_MASKED_LOGIT)
            z2[ch] = z

        def suffix_sums(ch):
            (r0, r1, nk), h = ch
            z = z2[ch]
            x = jnp.maximum(z, 0.0) + jnp.log(1.0 + jnp.exp2(-jnp.abs(z))) * LOG2E
            hi = x.astype(BF16)
            lo = (x - hi.astype(F32)).astype(BF16)
            nl[ch] = x
            later[ch] = _dot(jnp.concatenate([hi, lo], axis=1), ones[nk])

        def weights_times_values(ch):
            (r0, r1, nk), h = ch
            carry = carries[h][r0:r1]
            a = jnp.exp2(z2.pop(ch) - (nl[ch] + later[ch] + carry))
            acc_ref[h, r0:r1, :] += _dot(a.astype(BF16), v_ref[pl.ds(start, nk), pair(h)])
            fresh[ch] = carry + later.pop(ch)[:, 0:1] + nl.pop(ch)[:, 0:1]

        n = len(chains)
        for step in range(n + 2 * SB_STAGE_SKEW):
            if step < n:
                scores(chains[step])
            if 0 <= step - SB_STAGE_SKEW < n:
                suffix_sums(chains[step - SB_STAGE_SKEW])
            if 0 <= step - 2 * SB_STAGE_SKEW < n:
                weights_times_values(chains[step - 2 * SB_STAGE_SKEW])

        out = []
        for h in range(n_heads):
            pieces, row = [], 0
            for job in sorted(jobs):
                r0, r1, _ = job
                if r0 > row:
                    pieces.append(carries[h][row:r0])
                pieces.append(fresh[(job, h)])
                row = r1
            if row < tq:
                pieces.append(carries[h][row:tq])
            out.append(pieces[0] if len(pieces) == 1 else jnp.concatenate(pieces, axis=0))
        return tuple(out)

    zeros = jnp.zeros((tq, 1), F32)
    carries = block(i, (zeros,) * n_heads, [(0, tq, tk)], True)

    def least(carries):
        rows = functools.reduce(jnp.minimum, carries)
        below = jnp.min(rows[top:]) if top < tq else jnp.float32(SB_STICK_GONE_LOG2)
        return jnp.min(rows[:top]), below

    def walk(state, r1, watch):
        def live(state):
            return jnp.logical_and(state[0] < i, state[watch] < SB_STICK_GONE_LOG2)

        def body(state):
            n, _, _, carries = state
            carries = block(i - 1 - n, carries, [(0, r1, tk)], False)
            return (n + 1,) + least(carries) + (carries,)

        return lax.while_loop(live, body, state)

    state = (jnp.int32(0),) + least(carries) + (carries,)
    state = walk(state, tq, 2)
    walk(state, top, 1)
    for p in range(pairs):
        o_ref[:, p * LANES:(p + 1) * LANES] = jnp.where(
            lane < half, acc_ref[2 * p], acc_ref[2 * p + 1]).astype(o_ref.dtype)


def _sb_attention(sbq, sbkt, sbv, batch, seq):
    t = sbq.shape[0]
    tq, tk = min(SB_TQ, seq), min(SB_TK, seq)
    assert tq == tk and seq % tq == 0
    nq = seq // tq
    pairs = SB_PAIRS_PER_STEP
    width = pairs * LANES
    groups = SB_WIDTH // width
    return pl.pallas_call(
        functools.partial(_sb_kernel, tq=tq, tk=tk, pairs=pairs),
        grid=(batch, groups, nq),
        in_specs=[
            pl.BlockSpec((tq, width), lambda b, p, i: (b * nq + i, p)),
            pl.BlockSpec((width, seq), lambda b, p, i: (p, b)),
            pl.BlockSpec((seq, width), lambda b, p, i: (b, p)),
        ],
        out_specs=pl.BlockSpec((tq, width), lambda b, p, i: (b * nq + i, p)),
        out_shape=jax.ShapeDtypeStruct((t, SB_WIDTH), BF16),
        scratch_shapes=[pltpu.VMEM((2 * pairs, tq, LANES), F32)],
        compiler_params=_params(("parallel", "parallel", "arbitrary")),
        name="sb_attention",
    )(sbq, sbkt, sbv)


def _mlstm_kernel(qt_ref, k_ref, vt_ref, og_ref, grow_ref, brow_ref,
                  ghead_ref, o_ref, c_ref, m_ref, gate_ref, scan_ref, *, n_batch, n_chunks):
    L, d, H = ML_KERNEL_CHUNK, ML_HEAD_DIM, ML_HEADS
    chains = [(b, h) for b in range(n_batch) for h in range(H)]

    @pl.when(pl.program_id(1) == 0)
    def _():
        c_ref[...] = jnp.zeros_like(c_ref)
        m_ref[...] = jnp.zeros_like(m_ref)

    ss = lax.broadcasted_iota(jnp.int32, (L, L), 0)
    tt = lax.broadcasted_iota(jnp.int32, (L, L), 1)
    causal = ss <= tt
    eye = ss == tt
    gate_lane = lax.broadcasted_iota(jnp.int32, (2 * H, L), 1)
    gain_col = [jnp.sum(jnp.where(eye, ghead_ref[:, h * d:(h + 1) * d], 0.0), axis=1, keepdims=True)
                for h in range(H)]

    for b in range(n_batch):
        for c in range(n_chunks):
            lanes = slice(c * L, (c + 1) * L)
            g = grow_ref[b, :, lanes] + brow_ref[...]
            scan = _log_sigmoid(g)
            for k in range(L.bit_length() - 1):
                shifted = pltpu.roll(scan, 1 << k, axis=1)
                scan = scan + jnp.where(gate_lane >= (1 << k), shifted, 0.0)
            gate_ref[b, :, lanes] = g
            scan_ref[b, :, lanes] = scan

    def chunk(ci, ms):
        rows = pl.ds(pl.multiple_of(ci * L, L), L)
        lf_scan, g_row = {}, {}
        for b in range(n_batch):
            g_row[b] = gate_ref[b, :, rows]
            lf_scan[b] = scan_ref[b, :, rows]

        kb, qt, vt, c_st, s_raw, q_c = {}, {}, {}, {}, {}, {}
        for ch in chains:
            b, h = ch
            cols = slice(h * d, (h + 1) * d)
            kb[ch] = k_ref[b, rows, cols]
            qt[ch] = qt_ref[b, cols, rows]
            vt[ch] = vt_ref[b, cols, rows]
            c_st[ch] = c_ref[b * H + h]
            s_raw[ch] = _dot(kb[ch], qt[ch])
            q_c[ch] = _dot(c_st[ch].astype(BF16), qt[ch])

        li_row, bcum, g_col = {}, {}, {}
        for ch in chains:
            b, h = ch
            li_row[ch] = g_row[b][h:h + 1, :]
            bcum[ch] = lf_scan[b][H + h:H + h + 1, :]
            g_col[ch] = jnp.sum(jnp.where(eye, li_row[ch] - bcum[ch], 0.0), axis=1, keepdims=True)

        w_inter, m_t, intra, den, w_state, decay, new_ms = {}, {}, {}, {}, {}, {}, []
        for idx, ch in enumerate(chains):
            m_prev = ms[idx]
            dmat = jnp.where(causal, bcum[ch] + g_col.pop(ch), -jnp.inf)
            inter = bcum[ch] + m_prev
            m_t[ch] = jnp.maximum(jnp.max(dmat, axis=0, keepdims=True), inter)
            sc = s_raw.pop(ch) * jnp.exp(dmat - m_t[ch])
            w_inter[ch] = jnp.exp(inter - m_t[ch])
            den[ch] = jnp.sum(sc, axis=0, keepdims=True)
            intra[ch] = _dot(vt[ch], sc.astype(BF16))
            btot = bcum[ch][:, L - 1:L]
            m_new = m_t[ch][:, L - 1:L]
            w_state[ch] = jnp.exp(btot - bcum.pop(ch) + li_row.pop(ch) - m_new)
            decay[ch] = jnp.exp(btot + m_prev - m_new)
            new_ms.append(m_new)

        for ch in chains:
            b, h = ch
            w = w_state.pop(ch)
            lhs = jnp.concatenate([vt.pop(ch).astype(F32) * w,
                                   jnp.broadcast_to(w, (ML_STATE_PAD, L))], axis=0)
            c_ref[b * H + h] = decay.pop(ch) * c_st.pop(ch) + _dot(lhs.astype(BF16), kb.pop(ch))

        for ch in chains:
            b, h = ch
            cols = slice(h * d, (h + 1) * d)
            readout = q_c.pop(ch)
            num = intra.pop(ch) + w_inter[ch] * readout[0:d]
            total = den.pop(ch) + w_inter.pop(ch) * readout[d:d + 1]
            hval = num / jnp.maximum(jnp.abs(total), jnp.exp(-m_t.pop(ch)))
            ssq = jnp.mean(hval * hval, axis=0, keepdims=True)
            hn = (hval * lax.rsqrt(ssq + EPS) * gain_col[h]).T
            og = og_ref[b, rows, cols]
            o_ref[b, rows, cols] = (hn * jax.nn.sigmoid(og)).astype(o_ref.dtype)
        return tuple(new_ms)

    ms = tuple(m_ref[i] for i in range(len(chains)))
    ms = lax.fori_loop(0, n_chunks, chunk, ms)
    for i, m in enumerate(ms):
        m_ref[i] = m


def _mlstm(mlqt, mlk, mlvt, mlo, grow, brow, ghead):
    batch, seq, width = mlk.shape
    nb = min(ML_BATCH_PER_STEP, batch)
    sblk = min(ML_SEQ_BLOCK, seq)
    assert batch % nb == 0 and seq % sblk == 0 and sblk % ML_KERNEL_CHUNK == 0
    assert ML_KERNEL_CHUNK == ML_HEAD_DIM
    chunks = sblk // ML_KERNEL_CHUNK
    n_gate = 2 * ML_HEADS
    seq_blk = pl.BlockSpec((nb, sblk, width), lambda b, s: (b, s, 0))
    seq_blk_t = pl.BlockSpec((nb, width, sblk), lambda b, s: (b, 0, s))
    fixed2 = lambda b, s: (0, 0)
    return pl.pallas_call(
        functools.partial(_mlstm_kernel, n_batch=nb, n_chunks=chunks),
        grid=(batch // nb, seq // sblk),
        in_specs=[
            seq_blk_t, seq_blk, seq_blk_t, seq_blk,
            pl.BlockSpec((nb, n_gate, sblk), lambda b, s: (b, 0, s)),
            pl.BlockSpec(brow.shape, fixed2),
            pl.BlockSpec(ghead.shape, fixed2),
        ],
        out_specs=seq_blk,
        out_shape=jax.ShapeDtypeStruct((batch, seq, width), BF16),
        scratch_shapes=[pltpu.VMEM((nb * ML_HEADS, ML_HEAD_DIM + ML_STATE_PAD, ML_HEAD_DIM), F32),
                        pltpu.VMEM((nb * ML_HEADS, 1, 1), F32),
                        pltpu.VMEM((nb, n_gate, sblk), F32),
                        pltpu.VMEM((nb, n_gate, sblk), F32)],
        compiler_params=_params(("parallel", "arbitrary")),
        name="mlstm",
    )(mlqt, mlk, mlvt, mlo, grow, brow, ghead)


def _memkv_kernel(mem_ref, gm_ref, wk_ref, wv_ref, gk_ref, k_ref, v_ref):
    mn = _rms(mem_ref[...], gm_ref[...]).astype(BF16)
    k = _dot(mn, wk_ref[...])
    hd = k.shape[1] // X_HEADS
    for h in range(X_HEADS):
        k_ref[:, h * hd:(h + 1) * hd] = _rms(k[:, h * hd:(h + 1) * hd], gk_ref[...]).astype(BF16)
    v_ref[...] = _dot(mn, wv_ref[...]).astype(BF16)


def _memkv(mem2d, gm, wk, wv, gk, batch, layer):
    rows, d = mem2d.shape
    m = rows // batch
    blk = pl.BlockSpec((m, d), lambda b: (b, 0))
    fixed = lambda b: (0, 0)
    return pl.pallas_call(
        _memkv_kernel,
        grid=(batch,),
        in_specs=[blk, pl.BlockSpec((1, d), fixed), _layer_weight((d, d), layer),
                  _layer_weight((d, d), layer), pl.BlockSpec((1, d // X_HEADS), fixed)],
        out_specs=[blk, blk],
        out_shape=[jax.ShapeDtypeStruct((rows, d), BF16)] * 2,
        compiler_params=_params(("parallel",)),
        name="memkv",
    )(mem2d, gm.reshape(1, d), wk, wv, gk.reshape(1, d // X_HEADS))


def _post_kernel(x_ref, sb_ref, ml_ref, wo_ref, gx_ref, wq_ref, gq_ref, k_ref, v_ref, wxo_ref,
                 o_ref):
    tm, d = x_ref.shape
    hd = d // X_HEADS
    heads = [slice(h * hd, (h + 1) * hd) for h in range(X_HEADS)]
    parts = [slice(r, r + tm // POST_ROW_PARTS) for r in range(0, tm, tm // POST_ROW_PARTS)]
    x1, q, s, o = {}, {}, {}, {}
    for i, rows in enumerate(parts):
        x1[i] = (x_ref[rows, :] + _dot(sb_ref[rows, :], wo_ref[0:SB_WIDTH, :])
                 + _dot(ml_ref[rows, :], wo_ref[SB_WIDTH:SB_WIDTH + ML_WIDTH, :]))
    for i in range(len(parts)):
        q[i] = _dot(_rms(x1[i], gx_ref[...]).astype(BF16), wq_ref[...])
    for i in range(len(parts)):
        for h, cols in enumerate(heads):
            qn = _rms(q[i][:, cols], gq_ref[...]).astype(BF16)
            s[i, h] = _dot_nt(qn, k_ref[:, cols]) * (hd ** -0.5)
    for i in range(len(parts)):
        outs = []
        for h, cols in enumerate(heads):
            e = jnp.exp(s[i, h] - jnp.max(s[i, h], axis=-1, keepdims=True))
            part = _dot(e.astype(BF16), v_ref[:, cols]) / jnp.sum(e, axis=-1, keepdims=True)
            outs.append(part.astype(BF16))
        o[i] = jnp.concatenate(outs, axis=1)
    for i, rows in enumerate(parts):
        o_ref[rows, :] = x1[i] + _dot(o[i], wxo_ref[...])


def _post(x, sb, ml, wo, gx, wxq, gq, kmem, vmem, wxo, seq, layer):
    t, d = x.shape
    tm = min(POST_TM, seq)
    assert seq % tm == 0 and tm % POST_ROW_PARTS == 0
    per_seq = seq // tm
    m = kmem.shape[0] // (t // seq)
    row = lambda i: (i, 0)
    fixed = lambda i: (0, 0)
    mem_blk = pl.BlockSpec((m, d), lambda i: (i // per_seq, 0))
    return pl.pallas_call(
        _post_kernel,
        grid=(t // tm,),
        in_specs=[
            pl.BlockSpec((tm, d), row),
            pl.BlockSpec((tm, SB_WIDTH), row),
            pl.BlockSpec((tm, ML_WIDTH), row),
            _layer_weight((d, d), layer),
            pl.BlockSpec((1, d), fixed),
            _layer_weight((d, d), layer),
            pl.BlockSpec((1, d // X_HEADS), fixed),
            mem_blk, mem_blk,
            _layer_weight((d, d), layer),
        ],
        out_specs=pl.BlockSpec((tm, d), row),
        out_shape=jax.ShapeDtypeStruct((t, d), F32),
        compiler_params=_params(("parallel",)),
        name="post",
    )(x, sb, ml, wo, gx.reshape(1, d), wxq, gq.reshape(1, d // X_HEADS), kmem, vmem, wxo)


def _layer(x, mem2d, batch, seq, layer, p):
    l = layer
    x = _ffn(x, p["g_ff1"][l], p["w_ff1_gate"], p["w_ff1_up"], p["w_ff1_down"], l)

    s, w = SB_WIDTH, ML_WIDTH
    n_gate = 2 * ML_HEADS
    sbq, sbkt, sbv, mlqt, mlk, mlvt, mlo, gates = _proj(
        x, p["g_mix"][l], p["w_in_t"], p["w_conv"][l],
        p["b_conv"][l].reshape(1, 2 * w), seq, l)

    sb = _sb_attention(sbq, sbkt, sbv, batch, seq)

    seq3 = lambda a: a.reshape(batch, seq, w)
    ml = _mlstm(mlqt, seq3(mlk), mlvt, seq3(mlo), gates, p["b_gate"][l].reshape(n_gate, 1),
                p["g_mlstm_head"][l].reshape(1, w)).reshape(batch * seq, w)

    kmem, vmem = _memkv(mem2d, p["g_mem"][l], p["w_xk"], p["w_xv"], p["g_knorm"][l], batch, l)
    x = _post(x, sb, ml, p["w_out"], p["g_xattn"][l], p["w_xq"], p["g_qnorm"][l], kmem, vmem,
              p["w_xo"], seq, l)
    x = _ffn(x, p["g_ff2"][l], p["w_ff2_gate"], p["w_ff2_up"], p["w_ff2_down"], l)
    return x


_MATMUL_WEIGHTS = ("w_out", "w_xq", "w_xk", "w_xv", "w_xo")


def kernel(x, mem, g_ff1, w_ff1_gate, w_ff1_up, w_ff1_down, g_mix, w_in, b_gate, w_conv, b_conv, g_mlstm_head, w_out, g_xattn, g_mem, w_xq, w_xk, w_xv, g_qnorm, g_knorm, w_xo, g_ff2, w_ff2_gate, w_ff2_up, w_ff2_down):
    batch, seq, d = x.shape
    depth = g_ff1.shape[0]
    p = dict(g_ff1=g_ff1, w_ff1_gate=w_ff1_gate, w_ff1_up=w_ff1_up, w_ff1_down=w_ff1_down,
             g_mix=g_mix, w_in=w_in, b_gate=b_gate, w_conv=w_conv, b_conv=b_conv,
             g_mlstm_head=g_mlstm_head, w_out=w_out, g_xattn=g_xattn, g_mem=g_mem, w_xq=w_xq,
             w_xk=w_xk, w_xv=w_xv, g_qnorm=g_qnorm, g_knorm=g_knorm, w_xo=w_xo, g_ff2=g_ff2,
             w_ff2_gate=w_ff2_gate, w_ff2_up=w_ff2_up, w_ff2_down=w_ff2_down)
    for name in _MATMUL_WEIGHTS:
        p[name] = p[name].astype(BF16)
    p["w_in_t"] = w_in.transpose(0, 2, 1).astype(BF16)
    h = x.reshape(batch * seq, d)
    mem2d = mem.reshape(batch * mem.shape[1], d)
    for l in range(depth):
        h = _layer(h, mem2d, batch, seq, l, p)
    return h.reshape(batch, seq, d)
```

```python
import functools

import jax
import jax.numpy as jnp
from jax import lax
from jax.experimental import pallas as pl
from jax.experimental.pallas import tpu as pltpu

F32 = jnp.float32
BF16 = jnp.bfloat16

EPS = 1e-6
LOG2E = 1.4426950408889634
SB_HEADS = 8
SB_HEAD_DIM = 64
SB_WIDTH = SB_HEADS * SB_HEAD_DIM
ML_HEADS = 4
ML_HEAD_DIM = 128
ML_WIDTH = ML_HEADS * ML_HEAD_DIM
ML_KERNEL_CHUNK = 128
ML_STATE_PAD = 16
CONV_WIDTH = 4
X_HEADS = 4

V7X_VMEM_BYTES = 64 * 1024 * 1024
VMEM_LIMIT_BYTES = V7X_VMEM_BYTES - 8 * 1024 * 1024
LANES = 128
SUBLANES = 8

FFN_TM = 1024
FFN_TF = 256
PROJ_TM = 1024
POST_TM = 1024
POST_ROW_PARTS = 4
SB_TQ = 256
SB_TK = 256
SB_PAIRS_PER_STEP = 4
SB_STAGE_SKEW = 1
SB_STICK_GONE_LOG2 = 152.0
SB_MASKED_LOGIT = -1e30
SB_TOP_ROWS = 176
ML_BATCH_PER_STEP = 4
ML_SEQ_BLOCK = 512


def _params(semantics):
    return pltpu.CompilerParams(dimension_semantics=semantics,
                                vmem_limit_bytes=VMEM_LIMIT_BYTES)


def _rms(x, g):
    return x * lax.rsqrt(jnp.mean(x * x, axis=-1, keepdims=True) + EPS) * g


def _dot(a, b):
    return jnp.dot(a, b, preferred_element_type=F32)


def _dot_nt(a, b):
    return lax.dot_general(a, b, (((1,), (1,)), ((), ())), preferred_element_type=F32)


def _log_sigmoid(x):
    return -(jnp.maximum(-x, 0.0) + jnp.log1p(jnp.exp(-jnp.abs(x))))


def _layer_weight(shape, layer):
    return pl.BlockSpec((None,) + shape, lambda *_: (layer, 0, 0), pipeline_mode=pl.Buffered(1))


def _ffn_kernel(x_ref, g_ref, wg_hbm, wu_hbm, wd_hbm, o_ref,
                wg_ref, wu_ref, wd_ref, stage_g, stage_u, stage_d, sems, *, tf, layer):
    n_chunks = wg_ref.shape[1] // tf
    chunk = lambda c: slice(c * tf, (c + 1) * tf)

    @pl.when(pl.program_id(0) == 0)
    def _():
        def copies(c, slot):
            return (pltpu.make_async_copy(wg_hbm.at[layer, :, chunk(c)], stage_g.at[slot],
                                          sems.at[0, slot]),
                    pltpu.make_async_copy(wu_hbm.at[layer, :, chunk(c)], stage_u.at[slot],
                                          sems.at[1, slot]),
                    pltpu.make_async_copy(wd_hbm.at[layer, chunk(c), :], stage_d.at[slot],
                                          sems.at[2, slot]))

        for copy in copies(0, 0):
            copy.start()
        for c in range(n_chunks):
            slot = c % 2
            if c + 1 < n_chunks:
                for copy in copies(c + 1, 1 - slot):
                    copy.start()
            for copy in copies(c, slot):
                copy.wait()
            wg_ref[:, chunk(c)] = stage_g[slot].astype(BF16)
            wu_ref[:, chunk(c)] = stage_u[slot].astype(BF16)
            wd_ref[chunk(c), :] = stage_d[slot].astype(BF16)

    x = x_ref[...]
    xn = _rms(x, g_ref[...]).astype(BF16)
    hidden = []
    for c in range(n_chunks):
        gate = _dot(xn, wg_ref[:, chunk(c)])
        up = _dot(xn, wu_ref[:, chunk(c)])
        hidden.append((gate * jax.nn.sigmoid(gate) * up).astype(BF16))
    o_ref[...] = x + 0.5 * _dot(jnp.concatenate(hidden, axis=1), wd_ref[...])


def _ffn(x, g, wg, wu, wd, layer):
    t, d = x.shape
    f = wg.shape[2]
    tm, tf = min(FFN_TM, t), min(FFN_TF, f)
    assert t % tm == 0 and f % tf == 0
    hbm = pl.BlockSpec(memory_space=pl.ANY)
    return pl.pallas_call(
        functools.partial(_ffn_kernel, tf=tf, layer=layer),
        grid=(t // tm,),
        in_specs=[
            pl.BlockSpec((tm, d), lambda i: (i, 0)),
            pl.BlockSpec((1, d), lambda i: (0, 0)),
            hbm, hbm, hbm,
        ],
        out_specs=pl.BlockSpec((tm, d), lambda i: (i, 0)),
        out_shape=jax.ShapeDtypeStruct((t, d), F32),
        scratch_shapes=[
            pltpu.VMEM((d, f), BF16), pltpu.VMEM((d, f), BF16), pltpu.VMEM((f, d), BF16),
            pltpu.VMEM((2, d, tf), F32), pltpu.VMEM((2, d, tf), F32), pltpu.VMEM((2, tf, d), F32),
            pltpu.SemaphoreType.DMA((3, 2)),
        ],
        compiler_params=_params(("arbitrary",)),
        name="ffn",
    )(x, g.reshape(1, d), wg, wu, wd)


def _proj_kernel(x_ref, xprev_ref, g_ref, wint_ref, wconv_ref, bconv_ref,
                 sbq_ref, sbkt_ref, sbv_ref, mlqt_ref, mlk_ref, mlvt_ref, mlo_ref, gates_ref,
                 win_ref, *, tiles_per_seq):
    tm = x_ref.shape[0]
    s, w = SB_WIDTH, ML_WIDTH
    ml0 = 3 * s
    n_wide = ml0 + 4 * w

    @pl.when(pl.program_id(0) == 0)
    def _():
        for c0 in range(0, n_wide, s):
            win_ref[:, c0:c0 + s] = wint_ref[c0:c0 + s, :].astype(F32).T.astype(BF16)

    wq_ref = win_ref.at[:, 0:s]
    wk_ref = win_ref.at[:, s:2 * s]
    wv_ref = win_ref.at[:, 2 * s:3 * s]
    wml_ref = win_ref.at[:, ml0:ml0 + 2 * w]
    wmlv_ref = win_ref.at[:, ml0 + 2 * w:ml0 + 3 * w]
    wmlo_ref = win_ref.at[:, ml0 + 3 * w:ml0 + 4 * w]
    wgate_ref = wint_ref.at[n_wide:n_wide + 2 * ML_HEADS, :]
    u = _rms(x_ref[...], g_ref[...]).astype(BF16)
    u_prev = _rms(xprev_ref[...], g_ref[...]).astype(BF16)
    first = pl.program_id(0) % tiles_per_seq == 0

    def conv_silu(c0, c1):
        cur = _dot(u, wml_ref[:, c0:c1])
        prev = jnp.where(first, 0.0, _dot(u_prev, wml_ref[:, c0:c1]))
        ext = jnp.concatenate([prev, cur], axis=0)
        y = bconv_ref[:, c0:c1]
        for j in range(CONV_WIDTH):
            back = CONV_WIDTH - 1 - j
            window = pltpu.roll(ext, back, axis=0) if back else ext
            y = y + window[SUBLANES:, :] * wconv_ref[j:j + 1, c0:c1]
        return y * jax.nn.sigmoid(y)

    half = w // 2
    q_scale = ML_HEAD_DIM ** -0.5
    mlqt_ref[0:half, :] = (conv_silu(0, half) * q_scale).T.astype(BF16)
    sbq_ref[...] = (_dot(u, wq_ref[...]) * (LOG2E * SB_HEAD_DIM ** -0.5)).astype(BF16)
    mlqt_ref[half:w, :] = (conv_silu(half, w) * q_scale).T.astype(BF16)
    sbkt_ref[...] = _dot(u, wk_ref[...]).T.astype(BF16)
    mlk_ref[:, 0:half] = conv_silu(w, w + half).astype(BF16)
    sbv_ref[...] = _dot(u, wv_ref[...]).astype(BF16)
    mlk_ref[:, half:w] = conv_silu(w + half, 2 * w).astype(BF16)
    mlvt_ref[...] = _dot(u, wmlv_ref[...]).T.astype(BF16)
    mlo_ref[...] = _dot(u, wmlo_ref[...])
    gates_ref[...] = _dot_nt(wgate_ref[...], u)


def _proj(x, g, w_in_t, wconv, bconv, seq, layer):
    t, d = x.shape
    tm = min(PROJ_TM, seq)
    assert seq % tm == 0 and tm % SUBLANES == 0
    tiles_per_seq = seq // tm
    row = lambda i: (i, 0)
    fixed = lambda i: (0, 0)
    whole = lambda a: pl.BlockSpec(a.shape, fixed, pipeline_mode=pl.Buffered(1))
    seq_t = lambda rows: pl.BlockSpec((None, rows, tm),
                                      lambda i: (i // tiles_per_seq, 0, i % tiles_per_seq))
    return pl.pallas_call(
        functools.partial(_proj_kernel, tiles_per_seq=tiles_per_seq),
        grid=(t // tm,),
        in_specs=[
            pl.BlockSpec((tm, d), row),
            pl.BlockSpec((SUBLANES, d), lambda i: (jnp.maximum(i * (tm // SUBLANES) - 1, 0), 0)),
            pl.BlockSpec((1, d), fixed),
            _layer_weight(w_in_t.shape[1:], layer),
            whole(wconv), whole(bconv),
        ],
        out_specs=[
            pl.BlockSpec((tm, SB_WIDTH), row),
            pl.BlockSpec((SB_WIDTH, tm), lambda i: (0, i)),
            pl.BlockSpec((tm, SB_WIDTH), row),
            seq_t(ML_WIDTH),
            pl.BlockSpec((tm, ML_WIDTH), row),
            seq_t(ML_WIDTH),
            pl.BlockSpec((tm, ML_WIDTH), row),
            seq_t(2 * ML_HEADS),
        ],
        out_shape=[
            jax.ShapeDtypeStruct((t, SB_WIDTH), BF16),
            jax.ShapeDtypeStruct((SB_WIDTH, t), BF16),
            jax.ShapeDtypeStruct((t, SB_WIDTH), BF16),
            jax.ShapeDtypeStruct((t // seq, ML_WIDTH, seq), BF16),
            jax.ShapeDtypeStruct((t, ML_WIDTH), BF16),
            jax.ShapeDtypeStruct((t // seq, ML_WIDTH, seq), BF16),
            jax.ShapeDtypeStruct((t, ML_WIDTH), F32),
            jax.ShapeDtypeStruct((t // seq, 2 * ML_HEADS, seq), F32),
        ],
        scratch_shapes=[pltpu.VMEM((d, 3 * SB_WIDTH + 4 * ML_WIDTH), BF16)],
        compiler_params=_params(("arbitrary",)),
        name="proj",
    )(x, x, g.reshape(1, d), w_in_t, wconv, bconv)


def _sb_kernel(q_ref, kt_ref, v_ref, o_ref, acc_ref, *, tq, tk, pairs):
    i = pl.program_id(2)
    half = SB_HEAD_DIM
    n_heads = 2 * pairs
    top = min(SB_TOP_ROWS, tq)
    lane = lax.broadcasted_iota(jnp.int32, (tq, LANES), 1)
    q_heads = []
    for p in range(pairs):
        q = q_ref[:, p * LANES:(p + 1) * LANES]
        zero = jnp.zeros_like(q)
        q_heads += [jnp.where(lane < half, q, zero), jnp.where(lane >= half, q, zero)]
    pair = lambda h: slice((h // 2) * LANES, (h // 2 + 1) * LANES)

    def suffix_ones(n):
        r = lax.broadcasted_iota(jnp.int32, (2 * n, n), 0)
        c = lax.broadcasted_iota(jnp.int32, (2 * n, n), 1)
        return jnp.where((r & (n - 1)) > c, 1.0, 0.0).astype(BF16)

    ones = {tk: suffix_ones(tk)}

    acc_ref[...] = jnp.zeros_like(acc_ref)

    def block(kb, carries, jobs, masked):
        start = pl.multiple_of(kb * tk, tk)
        chains = [(job, h) for job in jobs for h in range(n_heads)]
        z2, nl, later, fresh = {}, {}, {}, {}

        def strict(job):
            r0, r1, nk = job
            tr = lax.broadcasted_iota(jnp.int32, (r1 - r0, nk), 0) + r0
            tc = lax.broadcasted_iota(jnp.int32, (r1 - r0, nk), 1)
            return tc < tr

        def scores(ch):
            (r0, r1, nk), h = ch
            z = _dot(q_heads[h][r0:r1], kt_ref[pair(h), pl.ds(start, nk)])
            if masked:
                z = jnp.where(strict(ch[0]), z, SB_MASKED_LOGIT)
            z2[ch] = z

        def suffix_sums(ch):
            (r0, r1, nk), h = ch
            z = z2[ch]
            x = jnp.maximum(z, 0.0) + jnp.log(1.0 + jnp.exp2(-jnp.abs(z))) * LOG2E
            hi = x.astype(BF16)
            lo = (x - hi.astype(F32)).astype(BF16)
            nl[ch] = x
            later[ch] = _dot(jnp.concatenate([hi, lo], axis=1), ones[nk])

        def weights_times_values(ch):
            (r0, r1, nk), h = ch
            carry = carries[h][r0:r1]
            a = jnp.exp2(z2.pop(ch) - (nl[ch] + later[ch] + carry))
            acc_ref[h, r0:r1, :] += _dot(a.astype(BF16), v_ref[pl.ds(start, nk), pair(h)])
            fresh[ch] = carry + later.pop(ch)[:, 0:1] + nl.pop(ch)[:, 0:1]

        n = len(chains)
        for step in range(n + 2 * SB_STAGE_SKEW):
            if step < n:
                scores(chains[step])
            if 0 <= step - SB_STAGE_SKEW < n:
                suffix_sums(chains[step - SB_STAGE_SKEW])
            if 0 <= step - 2 * SB_STAGE_SKEW < n:
                weights_times_values(chains[step - 2 * SB_STAGE_SKEW])

        out = []
        for h in range(n_heads):
            pieces, row = [], 0
            for job in sorted(jobs):
                r0, r1, _ = job
                if r0 > row:
                    pieces.append(carries[h][row:r0])
                pieces.append(fresh[(job, h)])
                row = r1
            if row < tq:
                pieces.append(carries[h][row:tq])
            out.append(pieces[0] if len(pieces) == 1 else jnp.concatenate(pieces, axis=0))
        return tuple(out)

    zeros = jnp.zeros((tq, 1), F32)
    carries = block(i, (zeros,) * n_heads, [(0, tq, tk)], True)

    def least(carries):
        rows = functools.reduce(jnp.minimum, carries)
        below = jnp.min(rows[top:]) if top < tq else jnp.float32(SB_STICK_GONE_LOG2)
        return jnp.min(rows[:top]), below

    def walk(state, r1, watch):
        def live(state):
            return jnp.logical_and(state[0] < i, state[watch] < SB_STICK_GONE_LOG2)

        def body(state):
            n, _, _, carries = state
            carries = block(i - 1 - n, carries, [(0, r1, tk)], False)
            return (n + 1,) + least(carries) + (carries,)

        return lax.while_loop(live, body, state)

    state = (jnp.int32(0),) + least(carries) + (carries,)
    state = walk(state, tq, 2)
    walk(state, top, 1)
    for p in range(pairs):
        o_ref[:, p * LANES:(p + 1) * LANES] = jnp.where(
            lane < half, acc_ref[2 * p], acc_ref[2 * p + 1]).astype(o_ref.dtype)


def _sb_attention(sbq, sbkt, sbv, batch, seq):
    t = sbq.shape[0]
    tq, tk = min(SB_TQ, seq), min(SB_TK, seq)
    assert tq == tk and seq % tq == 0
    nq = seq // tq
    pairs = SB_PAIRS_PER_STEP
    width = pairs * LANES
    groups = SB_WIDTH // width
    return pl.pallas_call(
        functools.partial(_sb_kernel, tq=tq, tk=tk, pairs=pairs),
        grid=(batch, groups, nq),
        in_specs=[
            pl.BlockSpec((tq, width), lambda b, p, i: (b * nq + i, p)),
            pl.BlockSpec((width, seq), lambda b, p, i: (p, b)),
            pl.BlockSpec((seq, width), lambda b, p, i: (b, p)),
        ],
        out_specs=pl.BlockSpec((tq, width), lambda b, p, i: (b * nq + i, p)),
        out_shape=jax.ShapeDtypeStruct((t, SB_WIDTH), BF16),
        scratch_shapes=[pltpu.VMEM((2 * pairs, tq, LANES), F32)],
        compiler_params=_params(("parallel", "parallel", "arbitrary")),
        name="sb_attention",
    )(sbq, sbkt, sbv)


def _mlstm_kernel(qt_ref, k_ref, vt_ref, og_ref, grow_ref, brow_ref,
                  ghead_ref, o_ref, c_ref, m_ref, gate_ref, scan_ref, *, n_batch, n_chunks):
    L, d, H = ML_KERNEL_CHUNK, ML_HEAD_DIM, ML_HEADS
    chains = [(b, h) for b in range(n_batch) for h in range(H)]

    @pl.when(pl.program_id(1) == 0)
    def _():
        c_ref[...] = jnp.zeros_like(c_ref)
        m_ref[...] = jnp.zeros_like(m_ref)

    ss = lax.broadcasted_iota(jnp.int32, (L, L), 0)
    tt = lax.broadcasted_iota(jnp.int32, (L, L), 1)
    causal = ss <= tt
    eye = ss == tt
    gate_lane = lax.broadcasted_iota(jnp.int32, (2 * H, L), 1)
    gain_col = [jnp.sum(jnp.where(eye, ghead_ref[:, h * d:(h + 1) * d], 0.0), axis=1, keepdims=True)
                for h in range(H)]

    for b in range(n_batch):
        for c in range(n_chunks):
            lanes = slice(c * L, (c + 1) * L)
            g = grow_ref[b, :, lanes] + brow_ref[...]
            scan = _log_sigmoid(g)
            for k in range(L.bit_length() - 1):
                shifted = pltpu.roll(scan, 1 << k, axis=1)
                scan = scan + jnp.where(gate_lane >= (1 << k), shifted, 0.0)
            gate_ref[b, :, lanes] = g
            scan_ref[b, :, lanes] = scan

    def chunk(ci, ms):
        rows = pl.ds(pl.multiple_of(ci * L, L), L)
        lf_scan, g_row = {}, {}
        for b in range(n_batch):
            g_row[b] = gate_ref[b, :, rows]
            lf_scan[b] = scan_ref[b, :, rows]

        kb, qt, vt, c_st, s_raw, q_c = {}, {}, {}, {}, {}, {}
        for ch in chains:
            b, h = ch
            cols = slice(h * d, (h + 1) * d)
            kb[ch] = k_ref[b, rows, cols]
            qt[ch] = qt_ref[b, cols, rows]
            vt[ch] = vt_ref[b, cols, rows]
            c_st[ch] = c_ref[b * H + h]
            s_raw[ch] = _dot(kb[ch], qt[ch])
            q_c[ch] = _dot(c_st[ch].astype(BF16), qt[ch])

        li_row, bcum, g_col = {}, {}, {}
        for ch in chains:
            b, h = ch
            li_row[ch] = g_row[b][h:h + 1, :]
            bcum[ch] = lf_scan[b][H + h:H + h + 1, :]
            g_col[ch] = jnp.sum(jnp.where(eye, li_row[ch] - bcum[ch], 0.0), axis=1, keepdims=True)

        w_inter, m_t, intra, den, w_state, decay, new_ms = {}, {}, {}, {}, {}, {}, []
        for idx, ch in enumerate(chains):
            m_prev = ms[idx]
            dmat = jnp.where(causal, bcum[ch] + g_col.pop(ch), -jnp.inf)
            inter = bcum[ch] + m_prev
            m_t[ch] = jnp.maximum(jnp.max(dmat, axis=0, keepdims=True), inter)
            sc = s_raw.pop(ch) * jnp.exp(dmat - m_t[ch])
            w_inter[ch] = jnp.exp(inter - m_t[ch])
            den[ch] = jnp.sum(sc, axis=0, keepdims=True)
            intra[ch] = _dot(vt[ch], sc.astype(BF16))
            btot = bcum[ch][:, L - 1:L]
            m_new = m_t[ch][:, L - 1:L]
            w_state[ch] = jnp.exp(btot - bcum.pop(ch) + li_row.pop(ch) - m_new)
            decay[ch] = jnp.exp(btot + m_prev - m_new)
            new_ms.append(m_new)

        for ch in chains:
            b, h = ch
            w = w_state.pop(ch)
            lhs = jnp.concatenate([vt.pop(ch).astype(F32) * w,
                                   jnp.broadcast_to(w, (ML_STATE_PAD, L))], axis=0)
            c_ref[b * H + h] = decay.pop(ch) * c_st.pop(ch) + _dot(lhs.astype(BF16), kb.pop(ch))

        for ch in chains:
            b, h = ch
            cols = slice(h * d, (h + 1) * d)
            readout = q_c.pop(ch)
            num = intra.pop(ch) + w_inter[ch] * readout[0:d]
            total = den.pop(ch) + w_inter.pop(ch) * readout[d:d + 1]
            hval = num / jnp.maximum(jnp.abs(total), jnp.exp(-m_t.pop(ch)))
            ssq = jnp.mean(hval * hval, axis=0, keepdims=True)
            hn = (hval * lax.rsqrt(ssq + EPS) * gain_col[h]).T
            og = og_ref[b, rows, cols]
            o_ref[b, rows, cols] = (hn * jax.nn.sigmoid(og)).astype(o_ref.dtype)
        return tuple(new_ms)

    ms = tuple(m_ref[i] for i in range(len(chains)))
    ms = lax.fori_loop(0, n_chunks, chunk, ms)
    for i, m in enumerate(ms):
        m_ref[i] = m


def _mlstm(mlqt, mlk, mlvt, mlo, grow, brow, ghead):
    batch, seq, width = mlk.shape
    nb = min(ML_BATCH_PER_STEP, batch)
    sblk = min(ML_SEQ_BLOCK, seq)
    assert batch % nb == 0 and seq % sblk == 0 and sblk % ML_KERNEL_CHUNK == 0
    assert ML_KERNEL_CHUNK == ML_HEAD_DIM
    chunks = sblk // ML_KERNEL_CHUNK
    n_gate = 2 * ML_HEADS
    seq_blk = pl.BlockSpec((nb, sblk, width), lambda b, s: (b, s, 0))
    seq_blk_t = pl.BlockSpec((nb, width, sblk), lambda b, s: (b, 0, s))
    fixed2 = lambda b, s: (0, 0)
    return pl.pallas_call(
        functools.partial(_mlstm_kernel, n_batch=nb, n_chunks=chunks),
        grid=(batch // nb, seq // sblk),
        in_specs=[
            seq_blk_t, seq_blk, seq_blk_t, seq_blk,
            pl.BlockSpec((nb, n_gate, sblk), lambda b, s: (b, 0, s)),
            pl.BlockSpec(brow.shape, fixed2),
            pl.BlockSpec(ghead.shape, fixed2),
        ],
        out_specs=seq_blk,
        out_shape=jax.ShapeDtypeStruct((batch, seq, width), BF16),
        scratch_shapes=[pltpu.VMEM((nb * ML_HEADS, ML_HEAD_DIM + ML_STATE_PAD, ML_HEAD_DIM), F32),
                        pltpu.VMEM((nb * ML_HEADS, 1, 1), F32),
                        pltpu.VMEM((nb, n_gate, sblk), F32),
                        pltpu.VMEM((nb, n_gate, sblk), F32)],
        compiler_params=_params(("parallel", "arbitrary")),
        name="mlstm",
    )(mlqt, mlk, mlvt, mlo, grow, brow, ghead)


def _memkv_kernel(mem_ref, gm_ref, wk_ref, wv_ref, gk_ref, k_ref, v_ref):
    mn = _rms(mem_ref[...], gm_ref[...]).astype(BF16)
    k = _dot(mn, wk_ref[...])
    hd = k.shape[1] // X_HEADS
    for h in range(X_HEADS):
        k_ref[:, h * hd:(h + 1) * hd] = _rms(k[:, h * hd:(h + 1) * hd], gk_ref[...]).astype(BF16)
    v_ref[...] = _dot(mn, wv_ref[...]).astype(BF16)


def _memkv(mem2d, gm, wk, wv, gk, batch, layer):
    rows, d = mem2d.shape
    m = rows // batch
    blk = pl.BlockSpec((m, d), lambda b: (b, 0))
    fixed = lambda b: (0, 0)
    return pl.pallas_call(
        _memkv_kernel,
        grid=(batch,),
        in_specs=[blk, pl.BlockSpec((1, d), fixed), _layer_weight((d, d), layer),
                  _layer_weight((d, d), layer), pl.BlockSpec((1, d // X_HEADS), fixed)],
        out_specs=[blk, blk],
        out_shape=[jax.ShapeDtypeStruct((rows, d), BF16)] * 2,
        compiler_params=_params(("parallel",)),
        name="memkv",
    )(mem2d, gm.reshape(1, d), wk, wv, gk.reshape(1, d // X_HEADS))


def _post_kernel(x_ref, sb_ref, ml_ref, wo_ref, gx_ref, wq_ref, gq_ref, k_ref, v_ref, wxo_ref,
                 o_ref):
    tm, d = x_ref.shape
    hd = d // X_HEADS
    heads = [slice(h * hd, (h + 1) * hd) for h in range(X_HEADS)]
    parts = [slice(r, r + tm // POST_ROW_PARTS) for r in range(0, tm, tm // POST_ROW_PARTS)]
    x1, q, s, o = {}, {}, {}, {}
    for i, rows in enumerate(parts):
        x1[i] = (x_ref[rows, :] + _dot(sb_ref[rows, :], wo_ref[0:SB_WIDTH, :])
                 + _dot(ml_ref[rows, :], wo_ref[SB_WIDTH:SB_WIDTH + ML_WIDTH, :]))
    for i in range(len(parts)):
        q[i] = _dot(_rms(x1[i], gx_ref[...]).astype(BF16), wq_ref[...])
    for i in range(len(parts)):
        for h, cols in enumerate(heads):
            qn = _rms(q[i][:, cols], gq_ref[...]).astype(BF16)
            s[i, h] = _dot_nt(qn, k_ref[:, cols]) * (hd ** -0.5)
    for i in range(len(parts)):
        outs = []
        for h, cols in enumerate(heads):
            e = jnp.exp(s[i, h] - jnp.max(s[i, h], axis=-1, keepdims=True))
            part = _dot(e.astype(BF16), v_ref[:, cols]) / jnp.sum(e, axis=-1, keepdims=True)
            outs.append(part.astype(BF16))
        o[i] = jnp.concatenate(outs, axis=1)
    for i, rows in enumerate(parts):
        o_ref[rows, :] = x1[i] + _dot(o[i], wxo_ref[...])


def _post(x, sb, ml, wo, gx, wxq, gq, kmem, vmem, wxo, seq, layer):
    t, d = x.shape
    tm = min(POST_TM, seq)
    assert seq % tm == 0 and tm % POST_ROW_PARTS == 0
    per_seq = seq // tm
    m = kmem.shape[0] // (t // seq)
    row = lambda i: (i, 0)
    fixed = lambda i: (0, 0)
    mem_blk = pl.BlockSpec((m, d), lambda i: (i // per_seq, 0))
    return pl.pallas_call(
        _post_kernel,
        grid=(t // tm,),
        in_specs=[
            pl.BlockSpec((tm, d), row),
            pl.BlockSpec((tm, SB_WIDTH), row),
            pl.BlockSpec((tm, ML_WIDTH), row),
            _layer_weight((d, d), layer),
            pl.BlockSpec((1, d), fixed),
            _layer_weight((d, d), layer),
            pl.BlockSpec((1, d // X_HEADS), fixed),
            mem_blk, mem_blk,
            _layer_weight((d, d), layer),
        ],
        out_specs=pl.BlockSpec((tm, d), row),
        out_shape=jax.ShapeDtypeStruct((t, d), F32),
        compiler_params=_params(("parallel",)),
        name="post",
    )(x, sb, ml, wo, gx.reshape(1, d), wxq, gq.reshape(1, d // X_HEADS), kmem, vmem, wxo)


def _layer(x, mem2d, batch, seq, layer, p):
    l = layer
    x = _ffn(x, p["g_ff1"][l], p["w_ff1_gate"], p["w_ff1_up"], p["w_ff1_down"], l)

    s, w = SB_WIDTH, ML_WIDTH
    n_gate = 2 * ML_HEADS
    sbq, sbkt, sbv, mlqt, mlk, mlvt, mlo, gates = _proj(
        x, p["g_mix"][l], p["w_in_t"], p["w_conv"][l],
        p["b_conv"][l].reshape(1, 2 * w), seq, l)

    sb = _sb_attention(sbq, sbkt, sbv, batch, seq)

    seq3 = lambda a: a.reshape(batch, seq, w)
    ml = _mlstm(mlqt, seq3(mlk), mlvt, seq3(mlo), gates, p["b_gate"][l].reshape(n_gate, 1),
                p["g_mlstm_head"][l].reshape(1, w)).reshape(batch * seq, w)

    kmem, vmem = _memkv(mem2d, p["g_mem"][l], p["w_xk"], p["w_xv"], p["g_knorm"][l], batch, l)
    x = _post(x, sb, ml, p["w_out"], p["g_xattn"][l], p["w_xq"], p["g_qnorm"][l], kmem, vmem,
              p["w_xo"], seq, l)
    x = _ffn(x, p["g_ff2"][l], p["w_ff2_gate"], p["w_ff2_up"], p["w_ff2_down"], l)
    return x


_MATMUL_WEIGHTS = ("w_out", "w_xq", "w_xk", "w_xv", "w_xo")


def kernel(x, mem, g_ff1, w_ff1_gate, w_ff1_up, w_ff1_down, g_mix, w_in, b_gate, w_conv, b_conv, g_mlstm_head, w_out, g_xattn, g_mem, w_xq, w_xk, w_xv, g_qnorm, g_knorm, w_xo, g_ff2, w_ff2_gate, w_ff2_up, w_ff2_down):
    batch, seq, d = x.shape
    depth = g_ff1.shape[0]
    p = dict(g_ff1=g_ff1, w_ff1_gate=w_ff1_gate, w_ff1_up=w_ff1_up, w_ff1_down=w_ff1_down,
             g_mix=g_mix, w_in=w_in, b_gate=b_gate, w_conv=w_conv, b_conv=b_conv,
             g_mlstm_head=g_mlstm_head, w_out=w_out, g_xattn=g_xattn, g_mem=g_mem, w_xq=w_xq,
             w_xk=w_xk, w_xv=w_xv, g_qnorm=g_qnorm, g_knorm=g_knorm, w_xo=w_xo, g_ff2=g_ff2,
             w_ff2_gate=w_ff2_gate, w_ff2_up=w_ff2_up, w_ff2_down=w_ff2_down)
    for name in _MATMUL_WEIGHTS:
        p[name] = p[name].astype(BF16)
    p["w_in_t"] = w_in.transpose(0, 2, 1).astype(BF16)
    h = x.reshape(batch * seq, d)
    mem2d = mem.reshape(batch * mem.shape[1], d)
    for l in range(depth):
        h = _layer(h, mem2d, batch, seq, l, p)
    return h.reshape(batch, seq, d)
```
